```python
import jax
import jax.numpy as jnp
from jax import lax
import numpy as np

D_MODEL = 1024
BATCH = 8
SEQ = 2048
DEPTH = 1

MIX_WIDTH = D_MODEL
ATT_WIDTH = MIX_WIDTH // 2
RWKV_WIDTH = MIX_WIDTH - ATT_WIDTH
HEAD_DIM = 64
ATT_HEADS = ATT_WIDTH // HEAD_DIM
RWKV_HEAD_SIZE = 64
RWKV_HEADS = RWKV_WIDTH // RWKV_HEAD_SIZE
DILATED_PATTERNS = ((128, 1), (512, 4), (2048, 16))
ATT_BLOCK = 128
D_FF = ((8 * D_MODEL // 3 + 127) // 128) * 128
DECAY_LORA = max(32, int(round(1.8 * RWKV_WIDTH ** 0.5 / 32)) * 32)
ICL_LORA = max(32, int(round(1.8 * RWKV_WIDTH ** 0.5 / 32)) * 32)
GATE_LORA = max(32, int(round(0.6 * RWKV_WIDTH ** 0.8 / 32)) * 32)
IN_SPLITS = (ATT_WIDTH, ATT_WIDTH, ATT_WIDTH, RWKV_WIDTH, RWKV_WIDTH, RWKV_WIDTH, RWKV_WIDTH)
IN_COLS = sum(IN_SPLITS)
FFN_RESIDUAL = 0.5
RMS_EPS = 1e-6
GN_EPS = 64e-5
NEG_INF = -1e30

kernel_name = 'hymba_dilated_rwkv7_macaron'


def rms_norm(x, gain):
    xf = x.astype(jnp.float32)
    y = xf * lax.rsqrt(jnp.mean(xf * xf, axis=-1, keepdims=True) + RMS_EPS)
    return (y * gain.astype(jnp.float32)).astype(x.dtype)


def swiglu(h, w_gate, w_up, w_down):
    return (jax.nn.silu(h @ w_gate) * (h @ w_up)) @ w_down


def token_shift_lerp(x, mu):
    prev = jnp.pad(x, ((0, 0), (1, 0), (0, 0)))[:, :-1]
    return x + (prev - x) * mu


def banded_causal_attention(q, k, v, window):
    b, g, length, dh = q.shape
    nb = -(-length // ATT_BLOCK)
    lp = nb * ATT_BLOCK
    qb = jnp.pad(q, ((0, 0), (0, 0), (0, lp - length), (0, 0))).reshape(b, g, nb, ATT_BLOCK, dh)
    kv_pad = ((0, 0), (0, 0), (ATT_BLOCK, lp - length), (0, 0))
    kb = jnp.pad(k, kv_pad).reshape(b, g, nb + 1, ATT_BLOCK, dh)
    vb = jnp.pad(v, kv_pad).reshape(b, g, nb + 1, ATT_BLOCK, dh)
    kw = jnp.concatenate([kb[:, :, :-1], kb[:, :, 1:]], axis=3)
    vw = jnp.concatenate([vb[:, :, :-1], vb[:, :, 1:]], axis=3)
    s = jnp.einsum('bgnqd,bgnkd->bgnqk', qb, kw).astype(jnp.float32) * (dh ** -0.5)
    qi = jnp.arange(ATT_BLOCK)[:, None]
    kj = jnp.arange(2 * ATT_BLOCK)[None, :]
    dist = ATT_BLOCK + qi - kj
    kpos = (jnp.arange(nb)[:, None, None] - 1) * ATT_BLOCK + kj[None]
    mask = (dist >= 0) & (dist <= window) & (kpos >= 0)
    s = jnp.where(mask, s, NEG_INF)
    lse = jax.nn.logsumexp(s, axis=-1)
    p = jnp.exp(s - lse[..., None])
    o = jnp.einsum('bgnqk,bgnkd->bgnqd', p.astype(v.dtype), vw)
    return (o.reshape(b, g, lp, dh)[:, :, :length], lse.reshape(b, g, lp)[:, :, :length])


def dilated_attention(q, k, v):
    b, h, s, dh = q.shape
    outs, lses = [], []
    for window, dil in DILATED_PATTERNS:
        length = s // dil

        def to_sub(t):
            return t.reshape(b, h, length, dil, dh).transpose(0, 1, 3, 2, 4).reshape(b, h * dil, length, dh)

        o, lse = banded_causal_attention(to_sub(q), to_sub(k), to_sub(v), window // dil)
        outs.append(o.reshape(b, h, dil, length, dh).transpose(0, 1, 3, 2, 4).reshape(b, h, s, dh))
        lses.append(lse.reshape(b, h, dil, length).transpose(0, 1, 3, 2).reshape(b, h, s))
    wts = jax.nn.softmax(jnp.stack(lses), axis=0)
    o = jnp.sum(wts[..., None] * jnp.stack(outs).astype(jnp.float32), axis=0)
    return o.astype(q.dtype)


def rwkv7_scan(r, decay, k, v, a, b):
    bsz, _, nh, n = r.shape

    def step(state, inp):
        r_t, w_t, k_t, v_t, a_t, b_t = inp
        sa = jnp.einsum('bhij,bhj->bhi', state, a_t)
        state = state * w_t[:, :, None, :] + sa[..., None] * b_t[:, :, None, :] + v_t[..., None] * k_t[:, :, None, :]
        return state, jnp.einsum('bhij,bhj->bhi', state, r_t)

    xs = tuple(jnp.moveaxis(t, 1, 0) for t in (r, decay, k, v, a, b))
    s0 = jnp.zeros((bsz, nh, n, n), jnp.float32)
    _, ys = lax.scan(step, s0, xs)
    return jnp.moveaxis(ys, 0, 1)


def rwkv7_time_mix(r_in, k_in, v_in, c_in, mu_r, mu_k, mu_v, mu_w, mu_a, mu_g,
                   w0, w1, w2, a0, a1, a2, g1, g2, k_k, k_a, r_k, ln_x_w, ln_x_b):
    bsz, s, c = r_in.shape
    f32 = jnp.float32

    def heads(t):
        return t.astype(f32).reshape(bsz, s, RWKV_HEADS, RWKV_HEAD_SIZE)

    r = token_shift_lerp(r_in, mu_r)
    k = token_shift_lerp(k_in, mu_k)
    v = token_shift_lerp(v_in, mu_v)
    cw = token_shift_lerp(c_in, mu_w)
    ca = token_shift_lerp(c_in, mu_a)
    cg = token_shift_lerp(c_in, mu_g)
    w_log = -jax.nn.softplus(-(w0 + jnp.tanh(cw @ w1) @ w2).astype(f32)) - 0.5
    decay = jnp.exp(-jnp.exp(w_log))
    a = jax.nn.sigmoid((a0 + (ca @ a1) @ a2).astype(f32))
    g = jax.nn.sigmoid(cg @ g1) @ g2
    kk = heads(k * k_k)
    kk = kk / jnp.maximum(jnp.linalg.norm(kk, axis=-1, keepdims=True), 1e-12)
    k = k.astype(f32) * (1.0 + (a - 1.0) * k_a.astype(f32))
    rh, kh, vh = heads(r), heads(k), heads(v)
    y = rwkv7_scan(rh, heads(decay), kh, vh, -kk, kk * heads(a))
    mean = jnp.mean(y, axis=-1, keepdims=True)
    var = jnp.mean(jnp.square(y - mean), axis=-1, keepdims=True)
    y = ((y - mean) * lax.rsqrt(var + GN_EPS)).reshape(bsz, s, c) * ln_x_w.astype(f32) + ln_x_b.astype(f32)
    bonus = jnp.sum(rh * kh * r_k.astype(f32), axis=-1, keepdims=True) * vh
    out = (y + bonus.reshape(bsz, s, c)) * g.astype(f32)
    return out.astype(r_in.dtype)


def setup_inputs(seed: int = 0) -> dict:
    key = jax.random.key(seed)
    ks = iter(jax.random.split(key, 40))
    f32 = jnp.float32

    def normal(shape, scale):
        return jax.random.normal(next(ks), shape, f32) * scale

    def gain(shape):
        return 1.0 + 0.02 * jax.random.normal(next(ks), shape, f32)

    def unif(shape, lo, hi):
        return jax.random.uniform(next(ks), shape, f32, lo, hi)

    L, D, RW = DEPTH, D_MODEL, RWKV_WIDTH
    return {
        'x': jax.random.normal(next(ks), (BATCH, SEQ, D), f32),
        'ffn1_norm': gain((L, D)),
        'ffn1_w_gate': normal((L, D, D_FF), D ** -0.5),
        'ffn1_w_up': normal((L, D, D_FF), D ** -0.5),
        'ffn1_w_down': normal((L, D_FF, D), D_FF ** -0.5),
        'mix_norm': gain((L, D)),
        'w_in': normal((L, D, IN_COLS), D ** -0.5),
        'q_norm': gain((L, HEAD_DIM)),
        'k_norm': gain((L, HEAD_DIM)),
        'mu_r': unif((L, RW), 0.0, 1.0),
        'mu_k': unif((L, RW), 0.0, 1.0),
        'mu_v': unif((L, RW), 0.0, 1.0),
        'mu_w': unif((L, RW), 0.0, 1.0),
        'mu_a': unif((L, RW), 0.0, 1.0),
        'mu_g': unif((L, RW), 0.0, 1.0),
        'w0': unif((L, RW), -6.5, -1.0),
        'w1': normal((L, RW, DECAY_LORA), RW ** -0.5),
        'w2': normal((L, DECAY_LORA, RW), 0.1 * DECAY_LORA ** -0.5),
        'a0': normal((L, RW), 0.1),
        'a1': normal((L, RW, ICL_LORA), RW ** -0.5),
        'a2': normal((L, ICL_LORA, RW), 0.1 * ICL_LORA ** -0.5),
        'g1': normal((L, RW, GATE_LORA), RW ** -0.5),
        'g2': normal((L, GATE_LORA, RW), GATE_LORA ** -0.5),
        'k_k': 0.85 + normal((L, RW), 0.05),
        'k_a': 1.0 + normal((L, RW), 0.05),
        'r_k': -0.04 + normal((L, RWKV_HEADS, RWKV_HEAD_SIZE), 0.02),
        'ln_x_w': gain((L, RW)),
        'ln_x_b': normal((L, RW), 0.02),
        'w_out': normal((L, MIX_WIDTH, D), MIX_WIDTH ** -0.5),
        'ffn2_norm': gain((L, D)),
        'ffn2_w_gate': normal((L, D, D_FF), D ** -0.5),
        'ffn2_w_up': normal((L, D, D_FF), D ** -0.5),
        'ffn2_w_down': normal((L, D_FF, D), D_FF ** -0.5),
    }


def reference(x, ffn1_norm, ffn1_w_gate, ffn1_w_up, ffn1_w_down, mix_norm, w_in, q_norm, k_norm,
              mu_r, mu_k, mu_v, mu_w, mu_a, mu_g, w0, w1, w2, a0, a1, a2, g1, g2,
              k_k, k_a, r_k, ln_x_w, ln_x_b, w_out, ffn2_norm, ffn2_w_gate, ffn2_w_up, ffn2_w_down):
    b, s, _ = x.shape
    split_idx = [int(i) for i in np.cumsum(IN_SPLITS)[:-1]]
    for l in range(DEPTH):
        h = rms_norm(x, ffn1_norm[l])
        x = x + FFN_RESIDUAL * swiglu(h, ffn1_w_gate[l], ffn1_w_up[l], ffn1_w_down[l])
        h = rms_norm(x, mix_norm[l])
        proj = h @ w_in[l]
        q, k, v, rr, rk, rv, rc = jnp.split(proj, split_idx, axis=-1)
        qh = rms_norm(q.reshape(b, s, ATT_HEADS, HEAD_DIM), q_norm[l]).transpose(0, 2, 1, 3)
        kh = rms_norm(k.reshape(b, s, ATT_HEADS, HEAD_DIM), k_norm[l]).transpose(0, 2, 1, 3)
        vh = v.reshape(b, s, ATT_HEADS, HEAD_DIM).transpose(0, 2, 1, 3)
        att = dilated_attention(qh, kh, vh).transpose(0, 2, 1, 3).reshape(b, s, ATT_WIDTH)
        rw = rwkv7_time_mix(rr, rk, rv, rc, mu_r[l], mu_k[l], mu_v[l], mu_w[l], mu_a[l], mu_g[l],
                            w0[l], w1[l], w2[l], a0[l], a1[l], a2[l], g1[l], g2[l],
                            k_k[l], k_a[l], r_k[l], ln_x_w[l], ln_x_b[l])
        x = x + jnp.concatenate([att, rw], axis=-1) @ w_out[l]
        h = rms_norm(x, ffn2_norm[l])
        x = x + FFN_RESIDUAL * swiglu(h, ffn2_w_gate[l], ffn2_w_up[l], ffn2_w_down[l])
    return x
```

```python
import functools
import math

import jax
import jax.numpy as jnp
import numpy as np
from jax import lax
from jax.experimental import pallas as pl
from jax.experimental.pallas import tpu as pltpu

F32 = jnp.float32
BF16 = jnp.bfloat16

HEAD_DIM = 64
DILATED_PATTERNS = ((128, 1), (512, 4), (2048, 16))
FFN_RESIDUAL = 0.5
RMS_EPS = 1e-6
GN_EPS = 64e-5
NEG_INF = -1e30
LORA_PAD = 128

V7X_VMEM_LIMIT_BYTES = 60000 * 1024

FFN_TOKENS = 512
IN_TOKENS = 512
ATT_BLOCK = 256
RWKV_TOKENS = 256
RWKV_CHUNK = 64
SLAB_HEADS = 4
SLAB = SLAB_HEADS * HEAD_DIM


def _bf(x):
    return x.astype(BF16)


def _dot(a, b):
    return jnp.dot(a, b, preferred_element_type=F32)


def _dot_nt(a, b):
    return lax.dot_general(a, b, (((1,), (1,)), ((), ())), preferred_element_type=F32)


def _dot_tn(a, b):
    return lax.dot_general(a, b, (((0,), (0,)), ((), ())), preferred_element_type=F32)


def _split_dot(x, w, parts):
    out = None
    rem = x
    for p in range(parts):
        piece = _bf(rem)
        term = _dot(piece, w)
        out = term if out is None else out + term
        if p + 1 < parts:
            rem = rem - piece.astype(F32)
    return out


def _split_dot_left(w, x, parts):
    out = None
    rem = x
    for p in range(parts):
        piece = _bf(rem)
        term = _dot(w, piece)
        out = term if out is None else out + term
        if p + 1 < parts:
            rem = rem - piece.astype(F32)
    return out


def _rms_norm_rows(x, gain):
    ms = jnp.mean(x * x, axis=-1, keepdims=True)
    return x * lax.rsqrt(ms + RMS_EPS) * gain


def _ffn_body(with_mix, n_ff_steps, *refs):
    if with_mix:
        (x_ref, att_ref, rw_ref, wo_a_ref, wo_r_ref, nw_ref, wg_ref, wu_ref, wd_ref,
         o_ref, xin_ref, h_ref, acc_ref) = refs
    else:
        x_ref, nw_ref, wg_ref, wu_ref, wd_ref, o_ref, xin_ref, h_ref, acc_ref = refs
    j = pl.program_id(1)

    @pl.when(j == 0)
    def _():
        x = x_ref[...]
        if with_mix:
            x = x + _dot(att_ref[...], wo_a_ref[...]) + _dot(rw_ref[...], wo_r_ref[...])
        xin_ref[...] = x
        h_ref[...] = _bf(_rms_norm_rows(x, nw_ref[...]))
        acc_ref[...] = jnp.zeros_like(acc_ref)

    h = h_ref[...]
    g = _dot(h, wg_ref[...])
    u = _dot(h, wu_ref[...])
    a = _bf(g * jax.nn.sigmoid(g) * u)
    acc_ref[...] += _dot(a, wd_ref[...])

    @pl.when(j == n_ff_steps - 1)
    def _():
        o_ref[...] = xin_ref[...] + FFN_RESIDUAL * acc_ref[...]


def _ffn_call(x, norm_w, wg, wu, wd, mix=None):
    t, d = x.shape
    d_ff = wg.shape[1]
    tm = FFN_TOKENS
    n_ff_steps = 2 if d_ff % 256 == 0 else 1
    tf = d_ff // n_ff_steps
    row = lambda i, j: (i, 0)
    fixed = lambda i, j: (0, 0)
    in_specs = [pl.BlockSpec((tm, d), row)]
    args = [x]
    if mix is not None:
        att, rw, wo_a, wo_r = mix
        in_specs += [pl.BlockSpec((tm, att.shape[1]), row), pl.BlockSpec((tm, rw.shape[1]), row),
                     pl.BlockSpec(wo_a.shape, fixed), pl.BlockSpec(wo_r.shape, fixed)]
        args += [att, rw, wo_a, wo_r]
    in_specs += [pl.BlockSpec((1, d), fixed),
                 pl.BlockSpec((d, tf), lambda i, j: (0, j)),
                 pl.BlockSpec((d, tf), lambda i, j: (0, j)),
                 pl.BlockSpec((tf, d), lambda i, j: (j, 0))]
    args += [norm_w.reshape(1, d), wg, wu, wd]
    return pl.pallas_call(
        functools.partial(_ffn_body, mix is not None, n_ff_steps),
        grid=(t // tm, n_ff_steps),
        in_specs=in_specs,
        out_specs=pl.BlockSpec((tm, d), row),
        out_shape=jax.ShapeDtypeStruct((t, d), F32),
        scratch_shapes=[pltpu.VMEM((tm, d), F32), pltpu.VMEM((tm, d), BF16), pltpu.VMEM((tm, d), F32)],
        compiler_params=pltpu.CompilerParams(
            dimension_semantics=("parallel", "arbitrary"), vmem_limit_bytes=V7X_VMEM_LIMIT_BYTES),
        name="ffn_mix" if mix is not None else "ffn",
    )(*args)


def _in_proj_body(att_width, x_ref, nw_ref, w_ref, qg_ref, kg_ref, hs_ref, q_ref, k_ref, v_ref, rin_ref):
    h = _bf(_rms_norm_rows(x_ref[...], nw_ref[...]))
    proj = _dot(h, w_ref[...])
    aw = att_width
    hs = hs_ref[...]

    def head_rms(z, gain):
        ms = _split_dot(z * z, hs, 2) * (1.0 / HEAD_DIM)
        return z * lax.rsqrt(ms + RMS_EPS) * gain

    q_ref[...] = _bf(head_rms(proj[:, 0:aw], qg_ref[...]) * (HEAD_DIM ** -0.5))
    k_ref[...] = _bf(head_rms(proj[:, aw:2 * aw], kg_ref[...]))
    v_ref[...] = _bf(proj[:, 2 * aw:3 * aw])
    rin_ref[...] = proj[:, 3 * aw:]


def _head_sum_matrix(width):
    head = np.arange(width) // HEAD_DIM
    return jnp.asarray(head[:, None] == head[None, :], dtype=BF16)


def _in_proj_call(x, norm_w, w_in, q_gain, k_gain, att_width):
    t, d = x.shape
    cols = w_in.shape[1]
    rw_cols = cols - 3 * att_width
    tm = IN_TOKENS
    heads = att_width // HEAD_DIM
    row = lambda i: (i, 0)
    fixed = lambda i: (0, 0)
    att_out = jax.ShapeDtypeStruct((t, att_width), BF16)
    return pl.pallas_call(
        functools.partial(_in_proj_body, att_width),
        grid=(t // tm,),
        in_specs=[pl.BlockSpec((tm, d), row), pl.BlockSpec((1, d), fixed), pl.BlockSpec((d, cols), fixed),
                  pl.BlockSpec((1, att_width), fixed), pl.BlockSpec((1, att_width), fixed),
                  pl.BlockSpec((att_width, att_width), fixed)],
        out_specs=[pl.BlockSpec((tm, att_width), row)] * 3 + [pl.BlockSpec((tm, rw_cols), row)],
        out_shape=[att_out, att_out, att_out, jax.ShapeDtypeStruct((t, rw_cols), F32)],
        compiler_params=pltpu.CompilerParams(
            dimension_semantics=("parallel",), vmem_limit_bytes=V7X_VMEM_LIMIT_BYTES),
        name="in_proj",
    )(x, norm_w.reshape(1, d), w_in, jnp.tile(q_gain, heads).reshape(1, att_width),
      jnp.tile(k_gain, heads).reshape(1, att_width), _head_sum_matrix(att_width))


def _multiplicity_table(seq):
    blk = ATT_BLOCK
    n = seq // blk
    dist = (np.arange(n)[:, None, None] * blk + np.arange(blk)[None, :, None]
            - np.arange(blk)[None, None, :])
    mult = np.zeros(dist.shape, np.float32)
    for window, dil in DILATED_PATTERNS:
        mult += (dist >= 0) & (dist <= window) & (dist % dil == 0)
    return jnp.asarray(mult)


def _attn_body(n_blocks, q_ref, k_ref, v_ref, mt_ref, o_ref):
    blk = ATT_BLOCK
    lanes = q_ref.shape[-1]
    lane_head = lax.broadcasted_iota(jnp.int32, (1, lanes), 1) // HEAD_DIM

    def q_block(i, carry):
        rows = pl.ds(pl.multiple_of(i * blk, blk), blk)
        q = q_ref[0, rows, :]
        out = jnp.zeros((blk, lanes), F32)
        for e in range(lanes // HEAD_DIM):
            own = lane_head == e
            qm = jnp.where(own, q, jnp.zeros_like(q))

            def k_block(j, state):
                m_run, l_run, acc = state
                cols = pl.ds(pl.multiple_of(j * blk, blk), blk)
                s = _dot_nt(qm, k_ref[0, cols, :])
                mult = mt_ref[i - j]
                s = jnp.where(mult > 0.0, s, NEG_INF)
                m_new = jnp.maximum(m_run, jnp.max(s, axis=1, keepdims=True))
                alpha = jnp.exp(m_run - m_new)
                p = jnp.exp(s - m_new) * mult
                l_new = alpha * l_run + jnp.sum(p, axis=1, keepdims=True)
                acc = alpha * acc + _dot(_bf(p), v_ref[0, cols, :])
                return m_new, l_new, acc

            init = (jnp.full((blk, 1), NEG_INF, F32), jnp.zeros((blk, 1), F32),
                    jnp.zeros((blk, lanes), F32))
            _, l_fin, acc = lax.fori_loop(0, i + 1, k_block, init)
            out = out + jnp.where(own, acc / l_fin, 0.0)
        o_ref[0, rows, :] = _bf(out)
        return carry

    lax.fori_loop(0, n_blocks, q_block, 0)


def _attn_call(q, k, v):
    b, s, w = q.shape
    lanes = 2 * HEAD_DIM
    n_blocks = s // ATT_BLOCK
    spec = pl.BlockSpec((1, s, lanes), lambda i, j: (i, 0, j))
    table = _multiplicity_table(s)
    return pl.pallas_call(
        functools.partial(_attn_body, n_blocks),
        grid=(b, w // lanes),
        in_specs=[spec, spec, spec, pl.BlockSpec(table.shape, lambda i, j: (0, 0, 0))],
        out_specs=spec,
        out_shape=jax.ShapeDtypeStruct((b, s, w), BF16),
        compiler_params=pltpu.CompilerParams(
            dimension_semantics=("parallel", "parallel"), vmem_limit_bytes=V7X_VMEM_LIMIT_BYTES),
        name="dilated_attn",
    )(q, k, v, table)


(P_MU_R, P_MU_K, P_MU_V, P_MU_W, P_MU_A, P_MU_G, P_W0, P_A0, P_KK, P_KA, P_RK, P_LNW, P_LNB) = range(13)
P_ROWS = 16


def _rwkv_consts(width):
    c = RWKV_CHUNK
    n_sub = RWKV_TOKENS // c
    t = np.arange(RWKV_TOKENS)
    tri = ((t[:, None] // c == t[None, :] // c) & (t[None, :] <= t[:, None]))
    blockdiag = (np.arange(SLAB)[:, None] // HEAD_DIM == np.arange(SLAB)[None, :] // HEAD_DIM)
    src = np.arange(SLAB_HEADS * c) % c
    tok = np.arange(c)[:, None]
    return (jnp.asarray(tri, BF16),
            _head_sum_matrix(width),
            jnp.asarray(blockdiag, F32),
            jnp.asarray(src[None, :] < tok, F32),
            jnp.asarray(src[None, :] <= tok, F32),
            jnp.asarray(src[None, :] == tok, F32))


def _rwkv_body(n_sub, rin_ref, p_ref, w1_ref, w2_ref, a1_ref, a2_ref, g1_ref, g2_ref,
               tri_ref, hs_ref, bd_ref, lt_ref, le_ref, eye_ref, o_ref, state_ref, prev_ref):
    c = RWKV_CHUNK
    tl = rin_ref.shape[1]
    w = o_ref.shape[-1]
    n_slabs = w // SLAB
    step = pl.program_id(1)

    @pl.when(step == 0)
    def _():
        state_ref[...] = jnp.zeros_like(state_ref)
        prev_ref[...] = jnp.zeros_like(prev_ref)

    x = rin_ref[0]
    first_row = lax.broadcasted_iota(jnp.int32, (tl, 1), 0) == 0
    x_prev = jnp.where(first_row, prev_ref[0:1, :], pltpu.roll(x, 1, 0))
    prev_ref[0:1, :] = rin_ref[0, tl - 1:tl, :]

    def par(row):
        return p_ref[row:row + 1, :]

    def sect(a, i):
        return a[:, i * w:(i + 1) * w]

    def lerp(i, mu_row):
        cur, prev = sect(x, i), sect(x_prev, i)
        return cur + (prev - cur) * par(mu_row)

    r = lerp(0, P_MU_R)
    k = lerp(1, P_MU_K)
    v = lerp(2, P_MU_V)
    cw = lerp(3, P_MU_W)
    ca = lerp(3, P_MU_A)
    cg = lerp(3, P_MU_G)

    hs = hs_ref[...]

    def head_sum(z):
        return _split_dot(z, hs, 2)

    zw = par(P_W0) + _dot(_bf(jnp.tanh(_dot(_bf(cw), w1_ref[...]))), w2_ref[...])
    log_w = -math.exp(-0.5) * jax.nn.sigmoid(zw)
    a_gate = jax.nn.sigmoid(par(P_A0) + _dot(_bf(_dot(_bf(ca), a1_ref[...])), a2_ref[...]))
    gate = _dot(_bf(jax.nn.sigmoid(_dot(_bf(cg), g1_ref[...]))), g2_ref[...])

    kk = k * par(P_KK)
    kk = kk / jnp.maximum(jnp.sqrt(head_sum(kk * kk)), 1e-12)
    k = k * (1.0 + (a_gate - 1.0) * par(P_KA))
    b_vec = kk * a_gate
    bonus = head_sum(r * k * par(P_RK)) * v

    cum = _split_dot_left(tri_ref[...], log_w, 3)
    bd = bd_ref[...]
    strictly_lower = lt_ref[...] > 0.0
    lower = le_ref[...] > 0.0
    eye = eye_ref[...]

    def block_diag(z):
        return _bf(jnp.concatenate([z] * SLAB_HEADS, axis=0) * bd)

    y_rows = []
    for ci in range(n_sub):
        rows = slice(ci * c, (ci + 1) * c)
        lw_c, cum_c = log_w[rows], cum[rows]
        total = jnp.sum(lw_c, axis=0, keepdims=True)
        dec_in = jnp.exp(cum_c)
        dec_ex = jnp.exp(cum_c - lw_c)
        grow = jnp.exp(-cum_c)
        dec_out = jnp.exp(total - cum_c)
        a_t = -kk[rows] * dec_ex
        r_t = r[rows] * dec_in
        b_t = b_vec[rows] * grow
        k_t = k[rows] * grow
        b_o = b_vec[rows] * dec_out
        k_o = k[rows] * dec_out
        v_c = v[rows]
        w_total = jnp.exp(total)
        y_slabs = []
        for si in range(n_slabs):
            ln = slice(si * SLAB, (si + 1) * SLAB)
            ar = _bf(jnp.concatenate([a_t[:, ln], r_t[:, ln]], axis=0))
            g_b = _dot_nt(ar, block_diag(b_t[:, ln]))
            g_k = _dot_nt(ar, block_diag(k_t[:, ln]))
            a_ab = jnp.where(strictly_lower, g_b[:c], 0.0)
            a_ak = jnp.where(strictly_lower, g_k[:c], 0.0)
            a_rb = jnp.where(lower, g_b[c:], 0.0)
            a_rk = jnp.where(lower, g_k[c:], 0.0)

            inv = eye + a_ab
            power = _dot(_bf(a_ab), block_diag(a_ab))
            covered = 2
            while covered * 2 < c:
                both = _dot(_bf(jnp.concatenate([inv, power], axis=0)), block_diag(power))
                inv = inv + both[:c]
                power = both[c:]
                covered *= 2
            inv = inv + _dot(_bf(inv), block_diag(power))

            state = state_ref[si]
            from_state = _dot_nt(ar, _bf(state))
            bd_v = block_diag(v_c[:, ln])
            u = _dot(_bf(inv), block_diag(from_state[:c] + _dot(_bf(a_ak), bd_v)))
            y_slabs.append(from_state[c:] + _dot(_bf(a_rb), block_diag(u)) + _dot(_bf(a_rk), bd_v))
            state_ref[si] = state * w_total[:, ln] + bd * (
                _dot_tn(_bf(u), _bf(b_o[:, ln])) + _dot_tn(_bf(v_c[:, ln]), _bf(k_o[:, ln])))
        y_rows.append(jnp.concatenate(y_slabs, axis=1))
    y = jnp.concatenate(y_rows, axis=0)

    inv_n = 1.0 / HEAD_DIM
    centered = y - head_sum(y) * inv_n
    var = head_sum(centered * centered) * inv_n
    y = centered * lax.rsqrt(var + GN_EPS) * par(P_LNW) + par(P_LNB)
    o_ref[0] = _bf((y + bonus) * gate)


def _pad_lora(down, up):
    rank = down.shape[1]
    return (_bf(jnp.pad(down, ((0, 0), (0, LORA_PAD - rank)))),
            _bf(jnp.pad(up, ((0, LORA_PAD - rank), (0, 0)))))


def _rwkv_call(rin, chan_params, loras):
    b, s, w4 = rin.shape
    w = w4 // 4
    tl = RWKV_TOKENS
    n_sub = tl // RWKV_CHUNK
    packed = jnp.zeros((P_ROWS, w), F32).at[:len(chan_params)].set(jnp.stack(chan_params))
    lora_args = []
    for down, up in loras:
        lora_args += list(_pad_lora(down, up))
    consts = _rwkv_consts(w)
    fixed = lambda i, j: (0, 0)
    return pl.pallas_call(
        functools.partial(_rwkv_body, n_sub),
        grid=(b, s // tl),
        in_specs=[pl.BlockSpec((1, tl, w4), lambda i, j: (i, j, 0)), pl.BlockSpec((P_ROWS, w), fixed)]
        + [pl.BlockSpec(a.shape, fixed) for a in lora_args]
        + [pl.BlockSpec(a.shape, fixed) for a in consts],
        out_specs=pl.BlockSpec((1, tl, w), lambda i, j: (i, j, 0)),
        out_shape=jax.ShapeDtypeStruct((b, s, w), BF16),
        scratch_shapes=[pltpu.VMEM((w // SLAB, SLAB, SLAB), F32), pltpu.VMEM((8, w4), F32)],
        compiler_params=pltpu.CompilerParams(
            dimension_semantics=("parallel", "arbitrary"), vmem_limit_bytes=V7X_VMEM_LIMIT_BYTES),
        name="rwkv7",
    )(rin, packed, *lora_args, *consts)


def kernel(x, ffn1_norm, ffn1_w_gate, ffn1_w_up, ffn1_w_down, mix_norm, w_in, q_norm, k_norm, mu_r, mu_k, mu_v, mu_w, mu_a, mu_g, w0, w1, w2, a0, a1, a2, g1, g2, k_k, k_a, r_k, ln_x_w, ln_x_b, w_out, ffn2_norm, ffn2_w_gate, ffn2_w_up, ffn2_w_down):
    b, s, d = x.shape
    depth = w_in.shape[0]
    rw_width = mu_r.shape[-1]
    att_width = (w_in.shape[-1] - 4 * rw_width) // 3
    h = x.reshape(b * s, d)
    for l in range(depth):
        h = _ffn_call(h, ffn1_norm[l], _bf(ffn1_w_gate[l]), _bf(ffn1_w_up[l]), _bf(ffn1_w_down[l]))
        q, k, v, rin = _in_proj_call(h, mix_norm[l], _bf(w_in[l]), q_norm[l], k_norm[l], att_width)
        att = _attn_call(q.reshape(b, s, att_width), k.reshape(b, s, att_width), v.reshape(b, s, att_width))
        chan = [mu_r[l], mu_k[l], mu_v[l], mu_w[l], mu_a[l], mu_g[l], w0[l], a0[l], k_k[l], k_a[l],
                r_k[l].reshape(rw_width), ln_x_w[l], ln_x_b[l]]
        rw = _rwkv_call(rin.reshape(b, s, 4 * rw_width), chan,
                        [(w1[l], w2[l]), (a1[l], a2[l]), (g1[l], g2[l])])
        w_o = _bf(w_out[l])
        h = _ffn_call(h, ffn2_norm[l], _bf(ffn2_w_gate[l]), _bf(ffn2_w_up[l]), _bf(ffn2_w_down[l]),
                      mix=(att.reshape(b * s, att_width), rw.reshape(b * s, rw_width),
                           w_o[:att_width], w_o[att_width:]))
    return h.reshape(b, s, d)
```

```python
import functools
import math

import jax
import jax.numpy as jnp
import numpy as np
from jax import lax
from jax.experimental import pallas as pl
from jax.experimental.pallas import tpu as pltpu

F32 = jnp.float32
BF16 = jnp.bfloat16

HEAD_DIM = 64
LANES = 128
DILATED_PATTERNS = ((128, 1), (512, 4), (2048, 16))
ATT_WINDOW = 128
FFN_RESIDUAL = 0.5
RMS_EPS = 1e-6
GN_EPS = 64e-5
NEG_INF = -1e30
LORA_PAD = 128

V7X_VMEM_LIMIT_BYTES = 60000 * 1024

FFN_TOKENS = 512
IN_TOKENS = 512
RWKV_TOKENS = 256
RWKV_CHUNK = 64
SLAB_HEADS = 4
SLAB = SLAB_HEADS * HEAD_DIM


def _bf(x):
    return x.astype(BF16)


def _dot(a, b):
    return jnp.dot(a, b, preferred_element_type=F32)


def _dot_nt(a, b):
    return lax.dot_general(a, b, (((1,), (1,)), ((), ())), preferred_element_type=F32)


def _dot_tn(a, b):
    return lax.dot_general(a, b, (((0,), (0,)), ((), ())), preferred_element_type=F32)


def _split_dot(x, w, parts):
    out = None
    rem = x
    for p in range(parts):
        piece = _bf(rem)
        term = _dot(piece, w)
        out = term if out is None else out + term
        if p + 1 < parts:
            rem = rem - piece.astype(F32)
    return out


def _split_dot_left(w, x, parts):
    out = None
    rem = x
    for p in range(parts):
        piece = _bf(rem)
        term = _dot(w, piece)
        out = term if out is None else out + term
        if p + 1 < parts:
            rem = rem - piece.astype(F32)
    return out


def _rms_norm_rows(x, gain):
    ms = jnp.mean(x * x, axis=-1, keepdims=True)
    return x * lax.rsqrt(ms + RMS_EPS) * gain


def _ffn_body(with_mix, n_ff_steps, *refs):
    if with_mix:
        (x_ref, att_ref, rw_ref, wo_a_ref, wo_r_ref, nw_ref, wg_ref, wu_ref, wd_ref,
         o_ref, xin_ref, h_ref, acc_ref) = refs
    else:
        x_ref, nw_ref, wg_ref, wu_ref, wd_ref, o_ref, xin_ref, h_ref, acc_ref = refs
    j = pl.program_id(1)

    @pl.when(j == 0)
    def _():
        x = x_ref[...]
        if with_mix:
            att = jnp.concatenate([att_ref[0, p] for p in range(att_ref.shape[1])], axis=1)
            x = x + _dot(att, wo_a_ref[...]) + _dot(rw_ref[...], wo_r_ref[...])
        xin_ref[...] = x
        h_ref[...] = _bf(_rms_norm_rows(x, nw_ref[...]))
        acc_ref[...] = jnp.zeros_like(acc_ref)

    h = h_ref[...]
    g = _dot(h, wg_ref[...])
    u = _dot(h, wu_ref[...])
    a = _bf(g * jax.nn.sigmoid(g) * u)
    acc_ref[...] += _dot(a, wd_ref[...])

    @pl.when(j == n_ff_steps - 1)
    def _():
        o_ref[...] = xin_ref[...] + FFN_RESIDUAL * acc_ref[...]


def _ffn_call(x, norm_w, wg, wu, wd, mix=None):
    t, d = x.shape
    d_ff = wg.shape[1]
    tm = FFN_TOKENS
    n_ff_steps = 2 if d_ff % 256 == 0 else 1
    tf = d_ff // n_ff_steps
    row = lambda i, j: (i, 0)
    fixed = lambda i, j: (0, 0)
    in_specs = [pl.BlockSpec((tm, d), row)]
    args = [x]
    if mix is not None:
        att, rw, wo_a, wo_r = mix
        _, pairs, s, _ = att.shape
        tiles = s // tm
        in_specs += [pl.BlockSpec((1, pairs, tm, LANES), lambda i, j: (i // tiles, 0, i % tiles, 0)),
                     pl.BlockSpec((tm, rw.shape[1]), row),
                     pl.BlockSpec(wo_a.shape, fixed), pl.BlockSpec(wo_r.shape, fixed)]
        args += [att, rw, wo_a, wo_r]
    in_specs += [pl.BlockSpec((1, d), fixed),
                 pl.BlockSpec((d, tf), lambda i, j: (0, j)),
                 pl.BlockSpec((d, tf), lambda i, j: (0, j)),
                 pl.BlockSpec((tf, d), lambda i, j: (j, 0))]
    args += [norm_w.reshape(1, d), wg, wu, wd]
    return pl.pallas_call(
        functools.partial(_ffn_body, mix is not None, n_ff_steps),
        grid=(t // tm, n_ff_steps),
        in_specs=in_specs,
        out_specs=pl.BlockSpec((tm, d), row),
        out_shape=jax.ShapeDtypeStruct((t, d), F32),
        scratch_shapes=[pltpu.VMEM((tm, d), F32), pltpu.VMEM((tm, d), BF16), pltpu.VMEM((tm, d), F32)],
        compiler_params=pltpu.CompilerParams(
            dimension_semantics=("parallel", "arbitrary"), vmem_limit_bytes=V7X_VMEM_LIMIT_BYTES),
        name="ffn_mix" if mix is not None else "ffn",
    )(*args)


def _in_proj_body(att_width, x_ref, nw_ref, w_ref, qg_ref, kg_ref, hs_ref, q_ref, k_ref, v_ref, rin_ref):
    h = _bf(_rms_norm_rows(x_ref[0], nw_ref[...]))
    proj = _dot(h, w_ref[...])
    aw = att_width
    hs = hs_ref[...]

    def head_rms(z, gain):
        ms = _split_dot(z * z, hs, 2) * (1.0 / HEAD_DIM)
        return z * lax.rsqrt(ms + RMS_EPS) * gain

    q = _bf(head_rms(proj[:, 0:aw], qg_ref[...]) * (HEAD_DIM ** -0.5))
    k = _bf(head_rms(proj[:, aw:2 * aw], kg_ref[...]))
    v = _bf(proj[:, 2 * aw:3 * aw])
    for p in range(aw // LANES):
        lanes = slice(p * LANES, (p + 1) * LANES)
        q_ref[0, p] = q[:, lanes]
        k_ref[0, p] = k[:, lanes]
        v_ref[0, p] = v[:, lanes]
    rin_ref[0] = proj[:, 3 * aw:]


def _head_sum_matrix(width):
    head = np.arange(width) // HEAD_DIM
    return jnp.asarray(head[:, None] == head[None, :], dtype=BF16)


def _in_proj_call(x, norm_w, w_in, q_gain, k_gain, att_width):
    b, s, d = x.shape
    cols = w_in.shape[1]
    rw_cols = cols - 3 * att_width
    tm = IN_TOKENS
    heads = att_width // HEAD_DIM
    pairs = att_width // LANES
    fixed = lambda i, j: (0, 0)
    att_out = jax.ShapeDtypeStruct((b, pairs, s, LANES), BF16)
    att_spec = pl.BlockSpec((1, pairs, tm, LANES), lambda i, j: (i, 0, j, 0))
    return pl.pallas_call(
        functools.partial(_in_proj_body, att_width),
        grid=(b, s // tm),
        in_specs=[pl.BlockSpec((1, tm, d), lambda i, j: (i, j, 0)), pl.BlockSpec((1, d), fixed),
                  pl.BlockSpec((d, cols), fixed),
                  pl.BlockSpec((1, att_width), fixed), pl.BlockSpec((1, att_width), fixed),
                  pl.BlockSpec((att_width, att_width), fixed)],
        out_specs=[att_spec] * 3 + [pl.BlockSpec((1, tm, rw_cols), lambda i, j: (i, j, 0))],
        out_shape=[att_out, att_out, att_out, jax.ShapeDtypeStruct((b, s, rw_cols), F32)],
        compiler_params=pltpu.CompilerParams(
            dimension_semantics=("parallel", "parallel"), vmem_limit_bytes=V7X_VMEM_LIMIT_BYTES),
        name="in_proj",
    )(x, norm_w.reshape(1, d), w_in, jnp.tile(q_gain, heads).reshape(1, att_width),
      jnp.tile(k_gain, heads).reshape(1, att_width), _head_sum_matrix(att_width))


def _attn_masks():
    w = ATT_WINDOW
    r = np.arange(w)[:, None]
    causal = np.arange(w)[None, :] <= r
    u = np.arange(2 * w)[None, :]
    band = (u >= r) & (u <= r + w)
    two_heads = lambda m: jnp.asarray(np.concatenate([m, m], axis=0), F32)
    return two_heads(causal), two_heads(band)


def _attn_body(seq, qn_ref, kn_ref, vn_ref, q4_ref, k4_ref, v4_ref, q16_ref, k16_ref, v16_ref,
               causal_ref, band_ref, o_ref, *part_refs):
    w = ATT_WINDOW
    lower_head = lax.broadcasted_iota(jnp.int32, (1, LANES), 1) < HEAD_DIM
    causal = causal_ref[...] > 0.0
    band = band_ref[...] > 0.0

    def pick(two):
        return jnp.where(lower_head, two[:w], two[w:])

    def block(q, k_win, v_win, mask):
        zero = jnp.zeros_like(q)
        qq = jnp.concatenate([jnp.where(lower_head, q, zero), jnp.where(lower_head, zero, q)], axis=0)
        s = jnp.where(mask, _dot_nt(qq, k_win), NEG_INF)
        m = jnp.max(s, axis=1, keepdims=True)
        p = jnp.exp(s - m)
        l = jnp.sum(p, axis=1, keepdims=True)
        o = _dot(_bf(p), v_win)
        l_full = pick(jnp.broadcast_to(l, (2 * w, LANES)))
        m_full = pick(jnp.broadcast_to(m, (2 * w, LANES)))
        return pick(o) / l_full, m_full + jnp.log(l_full)

    def run_pattern(q_ref, k_ref, v_ref, dil, out_ref, lse_ref):
        sub_len = seq // dil
        n_blocks = sub_len // w
        for c in range(dil):
            lanes = slice(c * LANES, (c + 1) * LANES)

            def store(nb, o, lse):
                rows = pl.ds(nb * (w * dil) + c, w, stride=dil) if dil > 1 else pl.ds(nb * w, w)
                out_ref[rows, :] = o
                lse_ref[rows, :] = lse

            first = slice(0, w)
            o, lse = block(q_ref[0, 0, first, lanes], k_ref[0, 0, first, lanes], v_ref[0, 0, first, lanes],
                           causal)
            store(0, o, lse)
            if n_blocks > 1:
                def later(nb, carry):
                    q_rows = pl.ds(pl.multiple_of(nb * w, w), w)
                    k_rows = pl.ds(pl.multiple_of((nb - 1) * w, w), 2 * w)
                    o, lse = block(q_ref[0, 0, q_rows, lanes], k_ref[0, 0, k_rows, lanes],
                                   v_ref[0, 0, k_rows, lanes], band)
                    store(nb, o, lse)
                    return carry

                lax.fori_loop(1, n_blocks, later, 0, unroll=3)

    views = ((qn_ref, kn_ref, vn_ref), (q4_ref, k4_ref, v4_ref), (q16_ref, k16_ref, v16_ref))
    for (window, dil), (q_ref, k_ref, v_ref), pi in zip(DILATED_PATTERNS, views, range(3)):
        assert window // dil == w
        run_pattern(q_ref, k_ref, v_ref, dil, part_refs[2 * pi], part_refs[2 * pi + 1])

    def merge(i, carry):
        rows = pl.ds(pl.multiple_of(i * w, w), w)
        outs = [part_refs[2 * pi][rows, :] for pi in range(3)]
        lses = [part_refs[2 * pi + 1][rows, :] for pi in range(3)]
        top = jnp.maximum(jnp.maximum(lses[0], lses[1]), lses[2])
        wts = [jnp.exp(l - top) for l in lses]
        o_ref[0, 0, rows, :] = _bf((wts[0] * outs[0] + wts[1] * outs[1] + wts[2] * outs[2])
                                   / (wts[0] + wts[1] + wts[2]))
        return carry

    lax.fori_loop(0, seq // w, merge, 0)


def _attn_call(q, k, v):
    b, pairs, s, _ = q.shape
    in_specs, args = [], []
    for _, dil in DILATED_PATTERNS:
        shape = (b, pairs, s // dil, dil * LANES)
        spec = pl.BlockSpec((1, 1) + shape[2:], lambda i, j: (i, j, 0, 0))
        for a in (q, k, v):
            in_specs.append(spec)
            args.append(a.reshape(shape))
    masks = _attn_masks()
    in_specs += [pl.BlockSpec(m.shape, lambda i, j: (0, 0)) for m in masks]
    return pl.pallas_call(
        functools.partial(_attn_body, s),
        grid=(b, pairs),
        in_specs=in_specs,
        out_specs=pl.BlockSpec((1, 1, s, LANES), lambda i, j: (i, j, 0, 0)),
        out_shape=jax.ShapeDtypeStruct((b, pairs, s, LANES), BF16),
        scratch_shapes=[pltpu.VMEM((s, LANES), F32)] * 6,
        compiler_params=pltpu.CompilerParams(
            dimension_semantics=("parallel", "parallel"), vmem_limit_bytes=V7X_VMEM_LIMIT_BYTES),
        name="dilated_attn",
    )(*args, *masks)


(P_MU_R, P_MU_K, P_MU_V, P_MU_W, P_MU_A, P_MU_G, P_W0, P_A0, P_KK, P_KA, P_RK, P_LNW, P_LNB) = range(13)
P_ROWS = 16


def _rwkv_consts(width):
    c = RWKV_CHUNK
    assert c == HEAD_DIM
    t = np.arange(RWKV_TOKENS)
    tri = ((t[:, None] // c == t[None, :] // c) & (t[None, :] <= t[:, None]))
    blockdiag = (np.arange(SLAB)[:, None] // HEAD_DIM == np.arange(SLAB)[None, :] // HEAD_DIM)
    src = np.arange(SLAB_HEADS * c) % c
    tok = np.arange(c)[:, None]
    return (jnp.asarray(tri, BF16),
            _head_sum_matrix(width),
            jnp.asarray(blockdiag, F32),
            jnp.asarray(src[None, :] < tok, F32),
            jnp.asarray(src[None, :] <= tok, F32),
            jnp.asarray(src[None, :] == tok, F32))


def _rwkv_body(n_sub, rin_ref, p_ref, w1_ref, w2_ref, a1_ref, a2_ref, g1_ref, g2_ref,
               tri_ref, hs_ref, bd_ref, lt_ref, le_ref, eye_ref, o_ref, state_ref, prev_ref):
    c = RWKV_CHUNK
    tl = rin_ref.shape[1]
    w = o_ref.shape[-1]
    n_slabs = w // SLAB
    step = pl.program_id(1)

    @pl.when(step == 0)
    def _():
        state_ref[...] = jnp.zeros_like(state_ref)
        prev_ref[...] = jnp.zeros_like(prev_ref)

    x = rin_ref[0]
    first_row = lax.broadcasted_iota(jnp.int32, (tl, 1), 0) == 0
    x_prev = jnp.where(first_row, prev_ref[0:1, :], pltpu.roll(x, 1, 0))
    prev_ref[0:1, :] = rin_ref[0, tl - 1:tl, :]

    def par(row):
        return p_ref[row:row + 1, :]

    def sect(a, i):
        return a[:, i * w:(i + 1) * w]

    def lerp(i, mu_row):
        cur, prev = sect(x, i), sect(x_prev, i)
        return cur + (prev - cur) * par(mu_row)

    r = lerp(0, P_MU_R)
    k = lerp(1, P_MU_K)
    v = lerp(2, P_MU_V)
    cw = lerp(3, P_MU_W)
    ca = lerp(3, P_MU_A)
    cg = lerp(3, P_MU_G)

    hs = hs_ref[...]

    def head_sum(z):
        return _split_dot(z, hs, 2)

    zw = par(P_W0) + _dot(_bf(jnp.tanh(_dot(_bf(cw), w1_ref[...]))), w2_ref[...])
    log_w = -math.exp(-0.5) * jax.nn.sigmoid(zw)
    a_gate = jax.nn.sigmoid(par(P_A0) + _dot(_bf(_dot(_bf(ca), a1_ref[...])), a2_ref[...]))
    gate = _dot(_bf(jax.nn.sigmoid(_dot(_bf(cg), g1_ref[...]))), g2_ref[...])

    kk = k * par(P_KK)
    kk = kk / jnp.maximum(jnp.sqrt(head_sum(kk * kk)), 1e-12)
    k = k * (1.0 + (a_gate - 1.0) * par(P_KA))
    b_vec = kk * a_gate
    bonus = head_sum(r * k * par(P_RK)) * v

    cum = _split_dot_left(tri_ref[...], log_w, 3)
    bd = bd_ref[...]
    strictly_lower = lt_ref[...] > 0.0
    lower = le_ref[...] > 0.0
    eye = eye_ref[...]

    def block_diag(z):
        return _bf(jnp.concatenate([z] * SLAB_HEADS, axis=0) * bd)

    y_rows = []
    for ci in range(n_sub):
        rows = slice(ci * c, (ci + 1) * c)
        lw_c, cum_c = log_w[rows], cum[rows]
        total = jnp.sum(lw_c, axis=0, keepdims=True)
        dec_in = jnp.exp(cum_c)
        dec_ex = jnp.exp(cum_c - lw_c)
        grow = jnp.exp(-cum_c)
        dec_out = jnp.exp(total - cum_c)
        a_t = -kk[rows] * dec_ex
        r_t = r[rows] * dec_in
        b_t = b_vec[rows] * grow
        k_t = k[rows] * grow
        b_o = b_vec[rows] * dec_out
        k_o = k[rows] * dec_out
        v_c = v[rows]
        w_total = jnp.exp(total)
        y_slabs = []
        for si in range(n_slabs):
            ln = slice(si * SLAB, (si + 1) * SLAB)
            ar = _bf(jnp.concatenate([a_t[:, ln], r_t[:, ln]], axis=0))
            g_b = _dot_nt(ar, block_diag(b_t[:, ln]))
            g_k = _dot_nt(ar, block_diag(k_t[:, ln]))
            a_ab = jnp.where(strictly_lower, g_b[:c], 0.0)
            a_ak = jnp.where(strictly_lower, g_k[:c], 0.0)
            a_rb = jnp.where(lower, g_b[c:], 0.0)
            a_rk = jnp.where(lower, g_k[c:], 0.0)

            inv = eye + a_ab
            power = _dot(_bf(a_ab), block_diag(a_ab))
            covered = 2
            while covered * 2 < c:
                both = _dot(_bf(jnp.concatenate([inv, power], axis=0)), block_diag(power))
                inv = inv + both[:c]
                power = both[c:]
                covered *= 2
            inv = inv + _dot(_bf(inv), block_diag(power))

            state = state_ref[si]
            from_state = _dot_nt(ar, _bf(state))
            bd_v = block_diag(v_c[:, ln])
            u = _dot(_bf(inv), block_diag(from_state[:c] + _dot(_bf(a_ak), bd_v)))
            y_slabs.append(from_state[c:] + _dot(_bf(a_rb), block_diag(u)) + _dot(_bf(a_rk), bd_v))
            state_ref[si] = state * w_total[:, ln] + bd * (
                _dot_tn(_bf(u), _bf(b_o[:, ln])) + _dot_tn(_bf(v_c[:, ln]), _bf(k_o[:, ln])))
        y_rows.append(jnp.concatenate(y_slabs, axis=1))
    y = jnp.concatenate(y_rows, axis=0)

    inv_n = 1.0 / HEAD_DIM
    centered = y - head_sum(y) * inv_n
    var = head_sum(centered * centered) * inv_n
    y = centered * lax.rsqrt(var + GN_EPS) * par(P_LNW) + par(P_LNB)
    o_ref[0] = _bf((y + bonus) * gate)


def _pad_lora(down, up):
    rank = down.shape[1]
    return (_bf(jnp.pad(down, ((0, 0), (0, LORA_PAD - rank)))),
            _bf(jnp.pad(up, ((0, LORA_PAD - rank), (0, 0)))))


def _rwkv_call(rin, chan_params, loras):
    b, s, w4 = rin.shape
    w = w4 // 4
    tl = RWKV_TOKENS
    n_sub = tl // RWKV_CHUNK
    packed = jnp.zeros((P_ROWS, w), F32).at[:len(chan_params)].set(jnp.stack(chan_params))
    lora_args = []
    for down, up in loras:
        lora_args += list(_pad_lora(down, up))
    consts = _rwkv_consts(w)
    fixed = lambda i, j: (0, 0)
    return pl.pallas_call(
        functools.partial(_rwkv_body, n_sub),
        grid=(b, s // tl),
        in_specs=[pl.BlockSpec((1, tl, w4), lambda i, j: (i, j, 0)), pl.BlockSpec((P_ROWS, w), fixed)]
        + [pl.BlockSpec(a.shape, fixed) for a in lora_args]
        + [pl.BlockSpec(a.shape, fixed) for a in consts],
        out_specs=pl.BlockSpec((1, tl, w), lambda i, j: (i, j, 0)),
        out_shape=jax.ShapeDtypeStruct((b, s, w), BF16),
        scratch_shapes=[pltpu.VMEM((w // SLAB, SLAB, SLAB), F32), pltpu.VMEM((8, w4), F32)],
        compiler_params=pltpu.CompilerParams(
            dimension_semantics=("parallel", "arbitrary"), vmem_limit_bytes=V7X_VMEM_LIMIT_BYTES),
        name="rwkv7",
    )(rin, packed, *lora_args, *consts)


def kernel(x, ffn1_norm, ffn1_w_gate, ffn1_w_up, ffn1_w_down, mix_norm, w_in, q_norm, k_norm, mu_r, mu_k, mu_v, mu_w, mu_a, mu_g, w0, w1, w2, a0, a1, a2, g1, g2, k_k, k_a, r_k, ln_x_w, ln_x_b, w_out, ffn2_norm, ffn2_w_gate, ffn2_w_up, ffn2_w_down):
    b, s, d = x.shape
    depth = w_in.shape[0]
    rw_width = mu_r.shape[-1]
    att_width = (w_in.shape[-1] - 4 * rw_width) // 3
    h = x.reshape(b * s, d)
    for l in range(depth):
        h = _ffn_call(h, ffn1_norm[l], _bf(ffn1_w_gate[l]), _bf(ffn1_w_up[l]), _bf(ffn1_w_down[l]))
        q, k, v, rin = _in_proj_call(h.reshape(b, s, d), mix_norm[l], _bf(w_in[l]), q_norm[l], k_norm[l],
                                     att_width)
        att = _attn_call(q, k, v)
        chan = [mu_r[l], mu_k[l], mu_v[l], mu_w[l], mu_a[l], mu_g[l], w0[l], a0[l], k_k[l], k_a[l],
                r_k[l].reshape(rw_width), ln_x_w[l], ln_x_b[l]]
        rw = _rwkv_call(rin, chan, [(w1[l], w2[l]), (a1[l], a2[l]), (g1[l], g2[l])])
        w_o = _bf(w_out[l])
        h = _ffn_call(h, ffn2_norm[l], _bf(ffn2_w_gate[l]), _bf(ffn2_w_up[l]), _bf(ffn2_w_down[l]),
                      mix=(att, rw.reshape(b * s, rw_width), w_o[:att_width], w_o[att_width:]))
    return h.reshape(b, s, d)
```

```python
import functools
import math

import jax
import jax.numpy as jnp
import numpy as np
from jax import lax
from jax.experimental import pallas as pl
from jax.experimental.pallas import tpu as pltpu

F32 = jnp.float32
BF16 = jnp.bfloat16

HEAD_DIM = 64
LANES = 128
DILATED_PATTERNS = ((128, 1), (512, 4), (2048, 16))
ATT_WINDOW = 128
FFN_RESIDUAL = 0.5
RMS_EPS = 1e-6
GN_EPS = 64e-5
NEG_INF = -1e30
LORA_PAD = 128

V7X_VMEM_LIMIT_BYTES = 60000 * 1024

FFN_TOKENS = 512
IN_TOKENS = 512
RWKV_TOKENS = 256
RWKV_SEQS = 2
RWKV_CHUNK = 64
SLAB_HEADS = 4
SLAB = SLAB_HEADS * HEAD_DIM


def _bf(x):
    return x.astype(BF16)


def _dot(a, b):
    return jnp.dot(a, b, preferred_element_type=F32)


def _dot_nt(a, b):
    return lax.dot_general(a, b, (((1,), (1,)), ((), ())), preferred_element_type=F32)


def _dot_tn(a, b):
    return lax.dot_general(a, b, (((0,), (0,)), ((), ())), preferred_element_type=F32)


def _split_dot(x, w, parts):
    out = None
    rem = x
    for p in range(parts):
        piece = _bf(rem)
        term = _dot(piece, w)
        out = term if out is None else out + term
        if p + 1 < parts:
            rem = rem - piece.astype(F32)
    return out


def _split_dot_left(w, x, parts):
    out = None
    rem = x
    for p in range(parts):
        piece = _bf(rem)
        term = _dot(w, piece)
        out = term if out is None else out + term
        if p + 1 < parts:
            rem = rem - piece.astype(F32)
    return out


def _rms_norm_rows(x, gain):
    ms = jnp.mean(x * x, axis=-1, keepdims=True)
    return x * lax.rsqrt(ms + RMS_EPS) * gain


def _ffn_body(with_mix, n_ff_steps, *refs):
    if with_mix:
        (x_ref, att_ref, rw_ref, wo_a_ref, wo_r_ref, nw_ref, wg_ref, wu_ref, wd_ref,
         o_ref, xin_ref, h_ref, acc_ref) = refs
    else:
        x_ref, nw_ref, wg_ref, wu_ref, wd_ref, o_ref, xin_ref, h_ref, acc_ref = refs
    j = pl.program_id(1)

    @pl.when(j == 0)
    def _():
        x = x_ref[...]
        if with_mix:
            att = jnp.concatenate([att_ref[0, p] for p in range(att_ref.shape[1])], axis=1)
            x = x + _dot(att, wo_a_ref[...]) + _dot(rw_ref[...], wo_r_ref[...])
        xin_ref[...] = x
        h_ref[...] = _bf(_rms_norm_rows(x, nw_ref[...]))
        acc_ref[...] = jnp.zeros_like(acc_ref)

    h = h_ref[...]
    g = _dot(h, wg_ref[...])
    u = _dot(h, wu_ref[...])
    a = _bf(g * jax.nn.sigmoid(g) * u)
    acc_ref[...] += _dot(a, wd_ref[...])

    @pl.when(j == n_ff_steps - 1)
    def _():
        o_ref[...] = xin_ref[...] + FFN_RESIDUAL * acc_ref[...]


def _ffn_call(x, norm_w, wg, wu, wd, mix=None):
    t, d = x.shape
    d_ff = wg.shape[1]
    tm = FFN_TOKENS
    n_ff_steps = 2 if d_ff % 256 == 0 else 1
    tf = d_ff // n_ff_steps
    row = lambda i, j: (i, 0)
    fixed = lambda i, j: (0, 0)
    in_specs = [pl.BlockSpec((tm, d), row)]
    args = [x]
    if mix is not None:
        att, rw, wo_a, wo_r = mix
        _, pairs, s, _ = att.shape
        tiles = s // tm
        in_specs += [pl.BlockSpec((1, pairs, tm, LANES), lambda i, j: (i // tiles, 0, i % tiles, 0)),
                     pl.BlockSpec((tm, rw.shape[1]), row),
                     pl.BlockSpec(wo_a.shape, fixed), pl.BlockSpec(wo_r.shape, fixed)]
        args += [att, rw, wo_a, wo_r]
    in_specs += [pl.BlockSpec((1, d), fixed),
                 pl.BlockSpec((d, tf), lambda i, j: (0, j)),
                 pl.BlockSpec((d, tf), lambda i, j: (0, j)),
                 pl.BlockSpec((tf, d), lambda i, j: (j, 0))]
    args += [norm_w.reshape(1, d), wg, wu, wd]
    return pl.pallas_call(
        functools.partial(_ffn_body, mix is not None, n_ff_steps),
        grid=(t // tm, n_ff_steps),
        in_specs=in_specs,
        out_specs=pl.BlockSpec((tm, d), row),
        out_shape=jax.ShapeDtypeStruct((t, d), F32),
        scratch_shapes=[pltpu.VMEM((tm, d), F32), pltpu.VMEM((tm, d), BF16), pltpu.VMEM((tm, d), F32)],
        compiler_params=pltpu.CompilerParams(
            dimension_semantics=("parallel", "arbitrary"), vmem_limit_bytes=V7X_VMEM_LIMIT_BYTES),
        name="ffn_mix" if mix is not None else "ffn",
    )(*args)


def _in_proj_body(att_width, x_ref, nw_ref, w_ref, qg_ref, kg_ref, hs_ref, *out_refs):
    view_refs, rin_ref, stage_ref = out_refs[:9], out_refs[9], out_refs[10]
    h = _bf(_rms_norm_rows(x_ref[0], nw_ref[...]))
    proj = _dot(h, w_ref[...])
    aw = att_width
    hs = hs_ref[...]
    tm = proj.shape[0]

    def head_rms(z, gain):
        ms = _split_dot(z * z, hs, 2) * (1.0 / HEAD_DIM)
        return z * lax.rsqrt(ms + RMS_EPS) * gain

    tensors = (head_rms(proj[:, 0:aw], qg_ref[...]) * (HEAD_DIM ** -0.5),
               head_rms(proj[:, aw:2 * aw], kg_ref[...]),
               proj[:, 2 * aw:3 * aw])
    for ti, val in enumerate(tensors):
        for p in range(aw // LANES):
            stage_ref[p] = val[:, p * LANES:(p + 1) * LANES]
        for (_, dil), ref in zip(DILATED_PATTERNS, view_refs[3 * ti:3 * ti + 3]):
            for p in range(aw // LANES):
                for c in range(dil):
                    rows = pl.ds(c, tm // dil, stride=dil) if dil > 1 else pl.ds(0, tm)
                    ref[0, p, :, c * LANES:(c + 1) * LANES] = _bf(stage_ref[p, rows, :])
    rin_ref[0] = proj[:, 3 * aw:]


def _head_sum_matrix(width):
    head = np.arange(width) // HEAD_DIM
    return jnp.asarray(head[:, None] == head[None, :], dtype=BF16)


def _in_proj_call(x, norm_w, w_in, q_gain, k_gain, att_width):
    b, s, d = x.shape
    cols = w_in.shape[1]
    rw_cols = cols - 3 * att_width
    tm = IN_TOKENS
    heads = att_width // HEAD_DIM
    pairs = att_width // LANES
    fixed = lambda i, j: (0, 0)
    view_shapes, view_specs = [], []
    for _, dil in DILATED_PATTERNS:
        view_shapes.append(jax.ShapeDtypeStruct((b, pairs, s // dil, dil * LANES), BF16))
        view_specs.append(pl.BlockSpec((1, pairs, tm // dil, dil * LANES), lambda i, j: (i, 0, j, 0)))
    return pl.pallas_call(
        functools.partial(_in_proj_body, att_width),
        grid=(b, s // tm),
        in_specs=[pl.BlockSpec((1, tm, d), lambda i, j: (i, j, 0)), pl.BlockSpec((1, d), fixed),
                  pl.BlockSpec((d, cols), fixed),
                  pl.BlockSpec((1, att_width), fixed), pl.BlockSpec((1, att_width), fixed),
                  pl.BlockSpec((att_width, att_width), fixed)],
        out_specs=view_specs * 3 + [pl.BlockSpec((1, tm, rw_cols), lambda i, j: (i, j, 0))],
        out_shape=view_shapes * 3 + [jax.ShapeDtypeStruct((b, s, rw_cols), F32)],
        scratch_shapes=[pltpu.VMEM((pairs, tm, LANES), F32)],
        compiler_params=pltpu.CompilerParams(
            dimension_semantics=("parallel", "parallel"), vmem_limit_bytes=V7X_VMEM_LIMIT_BYTES),
        name="in_proj",
    )(x, norm_w.reshape(1, d), w_in, jnp.tile(q_gain, heads).reshape(1, att_width),
      jnp.tile(k_gain, heads).reshape(1, att_width), _head_sum_matrix(att_width))


def _attn_masks():
    w = ATT_WINDOW
    r = np.arange(w)[:, None]
    causal = np.arange(w)[None, :] <= r
    u = np.arange(2 * w)[None, :]
    band = (u >= r) & (u <= r + w)
    two_heads = lambda m: jnp.asarray(np.concatenate([m, m], axis=0), F32)
    return two_heads(causal), two_heads(band)


def _attn_body(seq, qn_ref, kn_ref, vn_ref, q4_ref, k4_ref, v4_ref, q16_ref, k16_ref, v16_ref,
               causal_ref, band_ref, o_ref, *part_refs):
    w = ATT_WINDOW
    lower_head = lax.broadcasted_iota(jnp.int32, (1, LANES), 1) < HEAD_DIM
    causal = causal_ref[...] > 0.0
    band = band_ref[...] > 0.0

    def pick(two):
        return jnp.where(lower_head, two[:w], two[w:])

    def block(q, k_win, v_win, mask):
        zero = jnp.zeros_like(q)
        qq = jnp.concatenate([jnp.where(lower_head, q, zero), jnp.where(lower_head, zero, q)], axis=0)
        s = jnp.where(mask, _dot_nt(qq, k_win), NEG_INF)
        m = jnp.max(s, axis=1, keepdims=True)
        p = jnp.exp(s - m)
        l = jnp.sum(p, axis=1, keepdims=True)
        o = _dot(_bf(p), v_win)
        l_full = pick(jnp.broadcast_to(l, (2 * w, LANES)))
        m_full = pick(jnp.broadcast_to(m, (2 * w, LANES)))
        return pick(o) / l_full, m_full + jnp.log(l_full)

    def run_pattern(q_ref, k_ref, v_ref, dil, out_ref, lse_ref):
        sub_len = seq // dil
        n_blocks = sub_len // w
        for c in range(dil):
            lanes = slice(c * LANES, (c + 1) * LANES)

            def store(nb, o, lse):
                rows = pl.ds(nb * (w * dil) + c, w, stride=dil) if dil > 1 else pl.ds(nb * w, w)
                out_ref[rows, :] = o
                lse_ref[rows, :] = lse

            first = slice(0, w)
            o, lse = block(q_ref[0, 0, first, lanes], k_ref[0, 0, first, lanes], v_ref[0, 0, first, lanes],
                           causal)
            store(0, o, lse)
            if n_blocks > 1:
                def later(nb, carry):
                    q_rows = pl.ds(pl.multiple_of(nb * w, w), w)
                    k_rows = pl.ds(pl.multiple_of((nb - 1) * w, w), 2 * w)
                    o, lse = block(q_ref[0, 0, q_rows, lanes], k_ref[0, 0, k_rows, lanes],
                                   v_ref[0, 0, k_rows, lanes], band)
                    store(nb, o, lse)
                    return carry

                lax.fori_loop(1, n_blocks, later, 0, unroll=3)

    views = ((qn_ref, kn_ref, vn_ref), (q4_ref, k4_ref, v4_ref), (q16_ref, k16_ref, v16_ref))
    for (window, dil), (q_ref, k_ref, v_ref), pi in zip(DILATED_PATTERNS, views, range(3)):
        assert window // dil == w
        run_pattern(q_ref, k_ref, v_ref, dil, part_refs[2 * pi], part_refs[2 * pi + 1])

    def merge(i, carry):
        rows = pl.ds(pl.multiple_of(i * w, w), w)
        outs = [part_refs[2 * pi][rows, :] for pi in range(3)]
        lses = [part_refs[2 * pi + 1][rows, :] for pi in range(3)]
        top = jnp.maximum(jnp.maximum(lses[0], lses[1]), lses[2])
        wts = [jnp.exp(l - top) for l in lses]
        o_ref[0, 0, rows, :] = _bf((wts[0] * outs[0] + wts[1] * outs[1] + wts[2] * outs[2])
                                   / (wts[0] + wts[1] + wts[2]))
        return carry

    lax.fori_loop(0, seq // w, merge, 0)


def _attn_call(q_views, k_views, v_views):
    b, pairs, s, _ = q_views[0].shape
    in_specs, args = [], []
    for pi in range(len(DILATED_PATTERNS)):
        for views in (q_views, k_views, v_views):
            in_specs.append(pl.BlockSpec((1, 1) + views[pi].shape[2:], lambda i, j: (i, j, 0, 0)))
            args.append(views[pi])
    masks = _attn_masks()
    in_specs += [pl.BlockSpec(m.shape, lambda i, j: (0, 0)) for m in masks]
    return pl.pallas_call(
        functools.partial(_attn_body, s),
        grid=(b, pairs),
        in_specs=in_specs,
        out_specs=pl.BlockSpec((1, 1, s, LANES), lambda i, j: (i, j, 0, 0)),
        out_shape=jax.ShapeDtypeStruct((b, pairs, s, LANES), BF16),
        scratch_shapes=[pltpu.VMEM((s, LANES), F32)] * 6,
        compiler_params=pltpu.CompilerParams(
            dimension_semantics=("parallel", "parallel"), vmem_limit_bytes=V7X_VMEM_LIMIT_BYTES),
        name="dilated_attn",
    )(*args, *masks)


(P_MU_R, P_MU_K, P_MU_V, P_MU_W, P_MU_A, P_MU_G, P_W0, P_A0, P_KK, P_KA, P_RK, P_LNW, P_LNB) = range(13)
P_ROWS = 16


def _rwkv_consts(width, rows):
    c = RWKV_CHUNK
    assert c == HEAD_DIM
    t = np.arange(rows)
    tri = ((t[:, None] // c == t[None, :] // c) & (t[None, :] <= t[:, None]))
    blockdiag = (np.arange(SLAB)[:, None] // HEAD_DIM == np.arange(SLAB)[None, :] // HEAD_DIM)
    src = np.arange(SLAB_HEADS * c) % c
    tok = np.arange(c)[:, None]
    return (jnp.asarray(tri, BF16),
            _head_sum_matrix(width),
            jnp.asarray(blockdiag, F32),
            jnp.asarray(blockdiag, BF16),
            jnp.asarray(src[None, :] < tok, F32),
            jnp.asarray(src[None, :] <= tok, F32),
            jnp.asarray(src[None, :] == tok, F32))


def _rwkv_body(rin_ref, p_ref, w1_ref, w2_ref, a1_ref, a2_ref, g1_ref, g2_ref,
               tri_ref, hs_ref, bd_ref, bd16_ref, lt_ref, le_ref, eye_ref, o_ref, state_ref, prev_ref):
    c = RWKV_CHUNK
    nb, tl, w4 = rin_ref.shape
    w = o_ref.shape[-1]
    n_slabs = w // SLAB
    n_sub = tl // c
    step = pl.program_id(1)

    @pl.when(step == 0)
    def _():
        state_ref[...] = jnp.zeros_like(state_ref)
        prev_ref[...] = jnp.zeros_like(prev_ref)

    x = rin_ref[...].reshape(nb * tl, w4)
    row = lax.broadcasted_iota(jnp.int32, (nb * tl, 1), 0)
    x_prev = pltpu.roll(x, 1, 0)
    for bi in range(nb):
        x_prev = jnp.where(row == bi * tl, prev_ref[bi, 0:1, :], x_prev)
    for bi in range(nb):
        prev_ref[bi, 0:1, :] = rin_ref[bi, tl - 1:tl, :]

    def par(prow):
        return p_ref[prow:prow + 1, :]

    def sect(a, i):
        return a[:, i * w:(i + 1) * w]

    def lerp(i, mu_row):
        cur, prev = sect(x, i), sect(x_prev, i)
        return cur + (prev - cur) * par(mu_row)

    r = lerp(0, P_MU_R)
    k = lerp(1, P_MU_K)
    v = lerp(2, P_MU_V)
    cw = lerp(3, P_MU_W)
    ca = lerp(3, P_MU_A)
    cg = lerp(3, P_MU_G)

    hs = hs_ref[...]

    def head_sum(z):
        return _split_dot(z, hs, 2)

    zw = par(P_W0) + _dot(_bf(jnp.tanh(_dot(_bf(cw), w1_ref[...]))), w2_ref[...])
    log_w = -math.exp(-0.5) * jax.nn.sigmoid(zw)
    a_gate = jax.nn.sigmoid(par(P_A0) + _dot(_bf(_dot(_bf(ca), a1_ref[...])), a2_ref[...]))
    gate = _dot(_bf(jax.nn.sigmoid(_dot(_bf(cg), g1_ref[...]))), g2_ref[...])

    kk = k * par(P_KK)
    kk = kk / jnp.maximum(jnp.sqrt(head_sum(kk * kk)), 1e-12)
    k = k * (1.0 + (a_gate - 1.0) * par(P_KA))
    b_vec = kk * a_gate
    bonus = head_sum(r * k * par(P_RK)) * v

    cum = _split_dot_left(tri_ref[...], log_w, 3)
    bd = bd_ref[...]
    bd16 = bd16_ref[...]
    strictly_lower = lt_ref[...] > 0.0
    lower = le_ref[...] > 0.0
    eye = eye_ref[...]

    def block_diag(z):
        return jnp.concatenate([_bf(z)] * SLAB_HEADS, axis=0) * bd16

    probs = [(bi, ci, si) for bi in range(nb) for ci in range(n_sub) for si in range(n_slabs)]
    pre = {}
    for bi in range(nb):
        for ci in range(n_sub):
            rows = slice((bi * n_sub + ci) * c, (bi * n_sub + ci + 1) * c)
            lw_c, cum_c = log_w[rows], cum[rows]
            total = jnp.sum(lw_c, axis=0, keepdims=True)
            grow = jnp.exp(-cum_c)
            dec_out = jnp.exp(total - cum_c)
            a_t = -kk[rows] * jnp.exp(cum_c - lw_c)
            r_t = r[rows] * jnp.exp(cum_c)
            b_t, k_t = b_vec[rows] * grow, k[rows] * grow
            b_o, k_o = b_vec[rows] * dec_out, k[rows] * dec_out
            w_rows = jnp.broadcast_to(jnp.exp(total), (2 * c, w))
            for si in range(n_slabs):
                ln = slice(si * SLAB, (si + 1) * SLAB)
                ar = _bf(jnp.concatenate([a_t[:, ln], r_t[:, ln]], axis=0))
                g_b = _dot_nt(ar, block_diag(b_t[:, ln]))
                g_k = _dot_nt(ar, block_diag(k_t[:, ln]))
                a_k = jnp.concatenate([jnp.where(strictly_lower, g_k[:c], 0.0),
                                       jnp.where(lower, g_k[c:], 0.0)], axis=0)
                turned = jnp.transpose(jnp.concatenate([b_o[:, ln], k_o[:, ln], w_rows[:, ln]], axis=0))
                pre[bi, ci, si] = dict(
                    ar=ar, v=v[rows, ln],
                    a_ab=jnp.where(strictly_lower, g_b[:c], 0.0),
                    a_rb=_bf(jnp.where(lower, g_b[c:], 0.0)),
                    from_v=_dot(_bf(a_k), block_diag(v[rows, ln])),
                    out_t=_bf(turned[:, :2 * c]),
                    w_col=turned[:, 2 * c:2 * c + 1])

    inv = {p: eye + pre[p]["a_ab"] for p in probs}
    power = {p: _dot(_bf(pre[p]["a_ab"]), block_diag(pre[p]["a_ab"])) for p in probs}
    covered = 2
    while covered * 2 < c:
        for p in probs:
            both = _dot(_bf(jnp.concatenate([inv[p], power[p]], axis=0)), block_diag(power[p]))
            inv[p] = inv[p] + both[:c]
            power[p] = both[c:]
        covered *= 2
    for p in probs:
        inv[p] = _bf(inv[p] + _dot(_bf(inv[p]), block_diag(power[p])))

    states = {(bi, si): state_ref[bi * n_slabs + si] for bi in range(nb) for si in range(n_slabs)}
    y_parts = {}
    chains = list(states)
    for ci in range(n_sub):
        pp = {ch: pre[ch[0], ci, ch[1]] for ch in chains}
        from_state = {ch: _dot(pp[ch]["ar"], _bf(states[ch])) for ch in chains}
        rhs = {ch: block_diag(from_state[ch][:c] + pp[ch]["from_v"][:c]) for ch in chains}
        u = {ch: _dot(inv[ch[0], ci, ch[1]], rhs[ch]) for ch in chains}
        for ch in chains:
            states[ch] = states[ch] * pp[ch]["w_col"] + bd * _dot(
                pp[ch]["out_t"], _bf(jnp.concatenate([u[ch], pp[ch]["v"]], axis=0)))
        for ch in chains:
            y_parts[ch[0], ci, ch[1]] = (from_state[ch][c:] + pp[ch]["from_v"][c:]
                                         + _dot(pp[ch]["a_rb"], block_diag(u[ch])))
    for bi in range(nb):
        for si in range(n_slabs):
            state_ref[bi * n_slabs + si] = states[bi, si]
    y = jnp.concatenate(
        [jnp.concatenate([y_parts[bi, ci, si] for si in range(n_slabs)], axis=1)
         for bi in range(nb) for ci in range(n_sub)], axis=0)

    inv_n = 1.0 / HEAD_DIM
    centered = y - head_sum(y) * inv_n
    var = head_sum(centered * centered) * inv_n
    y = centered * lax.rsqrt(var + GN_EPS) * par(P_LNW) + par(P_LNB)
    o_ref[...] = _bf((y + bonus) * gate).reshape(nb, tl, w)


def _pad_lora(down, up):
    rank = down.shape[1]
    return (_bf(jnp.pad(down, ((0, 0), (0, LORA_PAD - rank)))),
            _bf(jnp.pad(up, ((0, LORA_PAD - rank), (0, 0)))))


def _rwkv_call(rin, chan_params, loras):
    b, s, w4 = rin.shape
    w = w4 // 4
    tl = RWKV_TOKENS
    nb = RWKV_SEQS if b % RWKV_SEQS == 0 else 1
    packed = jnp.zeros((P_ROWS, w), F32).at[:len(chan_params)].set(jnp.stack(chan_params))
    lora_args = []
    for down, up in loras:
        lora_args += list(_pad_lora(down, up))
    consts = _rwkv_consts(w, nb * tl)
    fixed = lambda i, j: (0, 0)
    return pl.pallas_call(
        _rwkv_body,
        grid=(b // nb, s // tl),
        in_specs=[pl.BlockSpec((nb, tl, w4), lambda i, j: (i, j, 0)), pl.BlockSpec((P_ROWS, w), fixed)]
        + [pl.BlockSpec(a.shape, fixed) for a in lora_args]
        + [pl.BlockSpec(a.shape, fixed) for a in consts],
        out_specs=pl.BlockSpec((nb, tl, w), lambda i, j: (i, j, 0)),
        out_shape=jax.ShapeDtypeStruct((b, s, w), BF16),
        scratch_shapes=[pltpu.VMEM((nb * (w // SLAB), SLAB, SLAB), F32), pltpu.VMEM((nb, 8, w4), F32)],
        compiler_params=pltpu.CompilerParams(
            dimension_semantics=("parallel", "arbitrary"), vmem_limit_bytes=V7X_VMEM_LIMIT_BYTES),
        name="rwkv7",
    )(rin, packed, *lora_args, *consts)


def kernel(x, ffn1_norm, ffn1_w_gate, ffn1_w_up, ffn1_w_down, mix_norm, w_in, q_norm, k_norm, mu_r, mu_k, mu_v, mu_w, mu_a, mu_g, w0, w1, w2, a0, a1, a2, g1, g2, k_k, k_a, r_k, ln_x_w, ln_x_b, w_out, ffn2_norm, ffn2_w_gate, ffn2_w_up, ffn2_w_down):
    b, s, d = x.shape
    depth = w_in.shape[0]
    rw_width = mu_r.shape[-1]
    att_width = (w_in.shape[-1] - 4 * rw_width) // 3
    h = x.reshape(b * s, d)
    for l in range(depth):
        h = _ffn_call(h, ffn1_norm[l], _bf(ffn1_w_gate[l]), _bf(ffn1_w_up[l]), _bf(ffn1_w_down[l]))
        outs = _in_proj_call(h.reshape(b, s, d), mix_norm[l], _bf(w_in[l]), q_norm[l], k_norm[l], att_width)
        att = _attn_call(outs[0:3], outs[3:6], outs[6:9])
        rin = outs[9]
        chan = [mu_r[l], mu_k[l], mu_v[l], mu_w[l], mu_a[l], mu_g[l], w0[l], a0[l], k_k[l], k_a[l],
                r_k[l].reshape(rw_width), ln_x_w[l], ln_x_b[l]]
        rw = _rwkv_call(rin, chan, [(w1[l], w2[l]), (a1[l], a2[l]), (g1[l], g2[l])])
        w_o = _bf(w_out[l])
        h = _ffn_call(h, ffn2_norm[l], _bf(ffn2_w_gate[l]), _bf(ffn2_w_up[l]), _bf(ffn2_w_down[l]),
                      mix=(att, rw.reshape(b * s, rw_width), w_o[:att_width], w_o[att_width:]))
    return h.reshape(b, s, d)
```

```python
import functools
import math

import jax
import jax.numpy as jnp
import numpy as np
from jax import lax
from jax.experimental import pallas as pl
from jax.experimental.pallas import tpu as pltpu

F32 = jnp.float32
BF16 = jnp.bfloat16

HEAD_DIM = 64
LANES = 128
DILATED_PATTERNS = ((128, 1), (512, 4), (2048, 16))
ATT_WINDOW = 128
FFN_RESIDUAL = 0.5
RMS_EPS = 1e-6
GN_EPS = 64e-5
NEG_INF = -1e30
LORA_PAD = 128

V7X_VMEM_LIMIT_BYTES = 60000 * 1024

FFN_TOKENS = 512
FFN_HIDDEN_CHUNK = 1024
IN_TOKENS = 512
RWKV_TOKENS = 256
RWKV_SEQS = 2
RWKV_CHUNK = 64
SLAB_HEADS = 4
SLAB = SLAB_HEADS * HEAD_DIM


def _bf(x):
    return x.astype(BF16)


def _dot(a, b):
    return jnp.dot(a, b, preferred_element_type=F32)


def _dot_nt(a, b):
    return lax.dot_general(a, b, (((1,), (1,)), ((), ())), preferred_element_type=F32)


def _dot_tn(a, b):
    return lax.dot_general(a, b, (((0,), (0,)), ((), ())), preferred_element_type=F32)


def _split_dot(x, w, parts):
    out = None
    rem = x
    for p in range(parts):
        piece = _bf(rem)
        term = _dot(piece, w)
        out = term if out is None else out + term
        if p + 1 < parts:
            rem = rem - piece.astype(F32)
    return out


def _split_dot_left(w, x, parts):
    out = None
    rem = x
    for p in range(parts):
        piece = _bf(rem)
        term = _dot(w, piece)
        out = term if out is None else out + term
        if p + 1 < parts:
            rem = rem - piece.astype(F32)
    return out


def _head_sums(z, membership, parts):
    g = membership.shape[0]
    return jnp.concatenate([_split_dot(z[:, i:i + g], membership, parts) for i in range(0, z.shape[1], g)],
                           axis=1)


def _rms_norm_rows(x, gain):
    ms = jnp.mean(x * x, axis=-1, keepdims=True)
    return x * lax.rsqrt(ms + RMS_EPS) * gain


def _ffn_body(with_mix, ff_chunks, *refs):
    if with_mix:
        x_ref, att_ref, rw_ref, wo_a_ref, wo_r_ref, nw_ref, wg_ref, wu_ref, wd_ref, o_ref = refs
    else:
        x_ref, nw_ref, wg_ref, wu_ref, wd_ref, o_ref = refs
    x = x_ref[...]
    if with_mix:
        att = jnp.concatenate([att_ref[0, p] for p in range(att_ref.shape[1])], axis=1)
        x = x + _dot(att, wo_a_ref[...]) + _dot(rw_ref[...], wo_r_ref[...])
    h = _bf(_rms_norm_rows(x, nw_ref[...]))
    acc = None
    for lo, hi in ff_chunks:
        g = _dot(h, wg_ref[:, lo:hi])
        u = _dot(h, wu_ref[:, lo:hi])
        part = _dot(_bf(g * jax.nn.sigmoid(g) * u), wd_ref[lo:hi, :])
        acc = part if acc is None else acc + part
    o_ref[...] = x + FFN_RESIDUAL * acc


def _ff_chunks(d_ff):
    bounds = list(range(0, d_ff, FFN_HIDDEN_CHUNK)) + [d_ff]
    return tuple(zip(bounds[:-1], bounds[1:]))


def _ffn_call(x, norm_w, wg, wu, wd, mix=None):
    t, d = x.shape
    tm = FFN_TOKENS
    row = lambda i: (i, 0)
    resident = lambda a: pl.BlockSpec(a.shape, lambda i: (0, 0), pipeline_mode=pl.Buffered(1))
    in_specs = [pl.BlockSpec((tm, d), row)]
    args = [x]
    if mix is not None:
        att, rw, wo_a, wo_r = mix
        _, pairs, s, _ = att.shape
        tiles = s // tm
        in_specs += [pl.BlockSpec((1, pairs, tm, LANES), lambda i: (i // tiles, 0, i % tiles, 0)),
                     pl.BlockSpec((tm, rw.shape[1]), row), resident(wo_a), resident(wo_r)]
        args += [att, rw, wo_a, wo_r]
    norm_w = norm_w.reshape(1, d)
    in_specs += [resident(norm_w), resident(wg), resident(wu), resident(wd)]
    args += [norm_w, wg, wu, wd]
    return pl.pallas_call(
        functools.partial(_ffn_body, mix is not None, _ff_chunks(wg.shape[1])),
        grid=(t // tm,),
        in_specs=in_specs,
        out_specs=pl.BlockSpec((tm, d), row),
        out_shape=jax.ShapeDtypeStruct((t, d), F32),
        compiler_params=pltpu.CompilerParams(
            dimension_semantics=("parallel",), vmem_limit_bytes=V7X_VMEM_LIMIT_BYTES),
        name="ffn_mix" if mix is not None else "ffn",
    )(*args)


def _in_proj_body(att_width, x_ref, nw_ref, w_ref, qg_ref, kg_ref, hs_ref, *out_refs):
    view_refs, rin_ref, stage_refs = out_refs[:9], out_refs[9], out_refs[10:]
    h = _bf(_rms_norm_rows(x_ref[0], nw_ref[...]))
    aw = att_width
    proj = _dot(h, w_ref[:, :3 * aw])
    hs = hs_ref[...]
    tm = proj.shape[0]

    def head_rms(z, gain):
        ms = _head_sums(z * z, hs, 2) * (1.0 / HEAD_DIM)
        return z * lax.rsqrt(ms + RMS_EPS) * gain

    tensors = (head_rms(proj[:, 0:aw], qg_ref[...]) * (HEAD_DIM ** -0.5),
               head_rms(proj[:, aw:2 * aw], kg_ref[...]),
               proj[:, 2 * aw:3 * aw])
    rw_cols = rin_ref.shape[-1]
    rw_step = rw_cols // (len(tensors) + 1)

    def rw_part(i):
        cols = slice(i * rw_step, (i + 1) * rw_step)
        rin_ref[0, :, cols] = _dot(h, w_ref[:, 3 * aw + cols.start:3 * aw + cols.stop])

    pairs = aw // LANES
    dils = [dil for _, dil in DILATED_PATTERNS]
    assert dils[0] == 1 and len(stage_refs) == len(dils) - 1
    rw_part(len(tensors))
    for ti, val in enumerate(tensors):
        rw_part(ti)
        for p in range(pairs):
            nat = val[:, p * LANES:(p + 1) * LANES]
            stage_refs[0][p] = nat
            view_refs[3 * ti][0, p] = _bf(nat)
        for li in range(1, len(dils)):
            prev_dil, dil = dils[li - 1], dils[li]
            ratio = dil // prev_dil
            for p in range(pairs):
                for cp in range(prev_dil):
                    for c2 in range(ratio):
                        c = cp + prev_dil * c2
                        blk = stage_refs[li - 1][p * prev_dil + cp, pl.ds(c2, tm // dil, stride=ratio), :]
                        if li + 1 < len(dils):
                            stage_refs[li][p * dil + c] = blk
                        view_refs[3 * ti + li][0, p, :, c * LANES:(c + 1) * LANES] = _bf(blk)


def _head_sum_matrix(width):
    head = np.arange(width) // HEAD_DIM
    return jnp.asarray(head[:, None] == head[None, :], dtype=BF16)


def _in_proj_call(x, norm_w, w_in, q_gain, k_gain, att_width):
    b, s, d = x.shape
    cols = w_in.shape[1]
    rw_cols = cols - 3 * att_width
    tm = IN_TOKENS
    heads = att_width // HEAD_DIM
    pairs = att_width // LANES
    fixed = lambda i, j: (0, 0)
    view_shapes, view_specs = [], []
    for _, dil in DILATED_PATTERNS:
        view_shapes.append(jax.ShapeDtypeStruct((b, pairs, s // dil, dil * LANES), BF16))
        view_specs.append(pl.BlockSpec((1, pairs, tm // dil, dil * LANES), lambda i, j: (i, 0, j, 0)))
    return pl.pallas_call(
        functools.partial(_in_proj_body, att_width),
        grid=(b, s // tm),
        in_specs=[pl.BlockSpec((1, tm, d), lambda i, j: (i, j, 0)), pl.BlockSpec((1, d), fixed),
                  pl.BlockSpec((d, cols), fixed, pipeline_mode=pl.Buffered(1)),
                  pl.BlockSpec((1, att_width), fixed), pl.BlockSpec((1, att_width), fixed),
                  pl.BlockSpec((SLAB, SLAB), fixed)],
        out_specs=view_specs * 3 + [pl.BlockSpec((1, tm, rw_cols), lambda i, j: (i, j, 0))],
        out_shape=view_shapes * 3 + [jax.ShapeDtypeStruct((b, s, rw_cols), F32)],
        scratch_shapes=[pltpu.VMEM((pairs * dil, tm // dil, LANES), F32) for _, dil in DILATED_PATTERNS[:-1]],
        compiler_params=pltpu.CompilerParams(
            dimension_semantics=("parallel", "parallel"), vmem_limit_bytes=V7X_VMEM_LIMIT_BYTES),
        name="in_proj",
    )(x, norm_w.reshape(1, d), w_in, jnp.tile(q_gain, heads).reshape(1, att_width),
      jnp.tile(k_gain, heads).reshape(1, att_width), _head_sum_matrix(SLAB))


def _attn_masks():
    w = ATT_WINDOW
    r = np.arange(w)[:, None]
    causal = np.arange(w)[None, :] <= r
    u = np.arange(2 * w)[None, :]
    band = (u >= r) & (u <= r + w)
    two_heads = lambda m: jnp.asarray(np.concatenate([m, m], axis=0), F32)
    return two_heads(causal), two_heads(band)


def _attn_body(seq, qn_ref, kn_ref, vn_ref, q4_ref, k4_ref, v4_ref, q16_ref, k16_ref, v16_ref,
               causal_ref, band_ref, o_ref, *part_refs):
    w = ATT_WINDOW
    lower_head = lax.broadcasted_iota(jnp.int32, (1, LANES), 1) < HEAD_DIM
    causal = causal_ref[...] > 0.0
    band = band_ref[...] > 0.0

    def pick(two):
        return jnp.where(lower_head, two[:w], two[w:])

    def block(q, k_win, v_win, mask):
        zero = jnp.zeros_like(q)
        qq = jnp.concatenate([jnp.where(lower_head, q, zero), jnp.where(lower_head, zero, q)], axis=0)
        s = jnp.where(mask, _dot_nt(qq, k_win), NEG_INF)
        m = jnp.max(s, axis=1, keepdims=True)
        p = jnp.exp(s - m)
        l = jnp.sum(p, axis=1, keepdims=True)
        o = _dot(_bf(p), v_win)
        l_full = pick(jnp.broadcast_to(l, (2 * w, LANES)))
        m_full = pick(jnp.broadcast_to(m, (2 * w, LANES)))
        return pick(o) / l_full, m_full + jnp.log(l_full)

    def run_pattern(q_ref, k_ref, v_ref, dil, out_ref, lse_ref):
        sub_len = seq // dil
        n_blocks = sub_len // w
        for c in range(dil):
            lanes = slice(c * LANES, (c + 1) * LANES)

            def store(nb, o, lse):
                rows = pl.ds(nb * (w * dil) + c, w, stride=dil) if dil > 1 else pl.ds(nb * w, w)
                out_ref[rows, :] = o
                lse_ref[rows, :] = lse

            first = slice(0, w)
            o, lse = block(q_ref[0, 0, first, lanes], k_ref[0, 0, first, lanes], v_ref[0, 0, first, lanes],
                           causal)
            store(0, o, lse)
            if n_blocks > 1:
                def later(nb, carry):
                    q_rows = pl.ds(pl.multiple_of(nb * w, w), w)
                    k_rows = pl.ds(pl.multiple_of((nb - 1) * w, w), 2 * w)
                    o, lse = block(q_ref[0, 0, q_rows, lanes], k_ref[0, 0, k_rows, lanes],
                                   v_ref[0, 0, k_rows, lanes], band)
                    store(nb, o, lse)
                    return carry

                lax.fori_loop(1, n_blocks, later, 0, unroll=3)

    views = ((qn_ref, kn_ref, vn_ref), (q4_ref, k4_ref, v4_ref), (q16_ref, k16_ref, v16_ref))
    for (window, dil), (q_ref, k_ref, v_ref), pi in zip(DILATED_PATTERNS, views, range(3)):
        assert window // dil == w
        run_pattern(q_ref, k_ref, v_ref, dil, part_refs[2 * pi], part_refs[2 * pi + 1])

    def merge(i, carry):
        rows = pl.ds(pl.multiple_of(i * w, w), w)
        outs = [part_refs[2 * pi][rows, :] for pi in range(3)]
        lses = [part_refs[2 * pi + 1][rows, :] for pi in range(3)]
        top = jnp.maximum(jnp.maximum(lses[0], lses[1]), lses[2])
        wts = [jnp.exp(l - top) for l in lses]
        o_ref[0, 0, rows, :] = _bf((wts[0] * outs[0] + wts[1] * outs[1] + wts[2] * outs[2])
                                   / (wts[0] + wts[1] + wts[2]))
        return carry

    lax.fori_loop(0, seq // w, merge, 0)


def _attn_call(q_views, k_views, v_views):
    b, pairs, s, _ = q_views[0].shape
    in_specs, args = [], []
    for pi in range(len(DILATED_PATTERNS)):
        for views in (q_views, k_views, v_views):
            in_specs.append(pl.BlockSpec((1, 1) + views[pi].shape[2:], lambda i, j: (i, j, 0, 0)))
            args.append(views[pi])
    masks = _attn_masks()
    in_specs += [pl.BlockSpec(m.shape, lambda i, j: (0, 0)) for m in masks]
    return pl.pallas_call(
        functools.partial(_attn_body, s),
        grid=(b, pairs),
        in_specs=in_specs,
        out_specs=pl.BlockSpec((1, 1, s, LANES), lambda i, j: (i, j, 0, 0)),
        out_shape=jax.ShapeDtypeStruct((b, pairs, s, LANES), BF16),
        scratch_shapes=[pltpu.VMEM((s, LANES), F32)] * 6,
        compiler_params=pltpu.CompilerParams(
            dimension_semantics=("parallel", "parallel"), vmem_limit_bytes=V7X_VMEM_LIMIT_BYTES),
        name="dilated_attn",
    )(*args, *masks)


(P_MU_R, P_MU_K, P_MU_V, P_MU_W, P_MU_A, P_MU_G, P_W0, P_A0, P_KK, P_KA, P_RK, P_LNW, P_LNB) = range(13)
P_ROWS = 16


def _rwkv_consts():
    c = RWKV_CHUNK
    assert c == HEAD_DIM
    t = np.arange(SLAB)
    tri = ((t[:, None] // c == t[None, :] // c) & (t[None, :] <= t[:, None]))
    blockdiag = (np.arange(SLAB)[:, None] // HEAD_DIM == np.arange(SLAB)[None, :] // HEAD_DIM)
    src = np.arange(SLAB_HEADS * c) % c
    tok = np.arange(c)[:, None]
    return (jnp.asarray(tri, BF16),
            jnp.asarray(blockdiag, F32),
            jnp.asarray(blockdiag, BF16),
            jnp.asarray(src[None, :] < tok, F32),
            jnp.asarray(src[None, :] <= tok, F32),
            jnp.asarray(src[None, :] == tok, F32))


def _rwkv_body(rin_ref, p_ref, w1_ref, w2_ref, a1_ref, a2_ref, g1_ref, g2_ref,
               tri_ref, bd_ref, bd16_ref, lt_ref, le_ref, eye_ref, o_ref, state_ref, prev_ref):
    c = RWKV_CHUNK
    nb, tl, w4 = rin_ref.shape
    w = o_ref.shape[-1]
    n_slabs = w // SLAB
    n_sub = tl // c
    step = pl.program_id(1)

    @pl.when(step == 0)
    def _():
        state_ref[...] = jnp.zeros_like(state_ref)
        prev_ref[...] = jnp.zeros_like(prev_ref)

    x = rin_ref[...].reshape(nb * tl, w4)
    row = lax.broadcasted_iota(jnp.int32, (nb * tl, 1), 0)
    x_prev = pltpu.roll(x, 1, 0)
    for bi in range(nb):
        x_prev = jnp.where(row == bi * tl, prev_ref[bi, 0:1, :], x_prev)
    for bi in range(nb):
        prev_ref[bi, 0:1, :] = rin_ref[bi, tl - 1:tl, :]

    def par(prow):
        return p_ref[prow:prow + 1, :]

    def sect(a, i):
        return a[:, i * w:(i + 1) * w]

    def lerp(i, mu_row):
        cur, prev = sect(x, i), sect(x_prev, i)
        return cur + (prev - cur) * par(mu_row)

    r = lerp(0, P_MU_R)
    k = lerp(1, P_MU_K)
    v = lerp(2, P_MU_V)
    cw = lerp(3, P_MU_W)
    ca = lerp(3, P_MU_A)
    cg = lerp(3, P_MU_G)

    bd = bd_ref[...]
    bd16 = bd16_ref[...]

    def head_sum(z):
        return _head_sums(z, bd16, 2)

    zw = par(P_W0) + _dot(_bf(jnp.tanh(_dot(_bf(cw), w1_ref[...]))), w2_ref[...])
    log_w = -math.exp(-0.5) * jax.nn.sigmoid(zw)
    a_gate = jax.nn.sigmoid(par(P_A0) + _dot(_bf(_dot(_bf(ca), a1_ref[...])), a2_ref[...]))
    gate = _dot(_bf(jax.nn.sigmoid(_dot(_bf(cg), g1_ref[...]))), g2_ref[...])

    kk = k * par(P_KK)
    kk = kk / jnp.maximum(jnp.sqrt(head_sum(kk * kk)), 1e-12)
    k = k * (1.0 + (a_gate - 1.0) * par(P_KA))
    b_vec = kk * a_gate
    bonus = head_sum(r * k * par(P_RK)) * v

    tri = tri_ref[...]
    cum = jnp.concatenate([_split_dot_left(tri, log_w[i:i + SLAB], 3) for i in range(0, nb * tl, SLAB)], axis=0)
    strictly_lower = lt_ref[...] > 0.0
    lower = le_ref[...] > 0.0
    eye = eye_ref[...]

    def block_diag(z):
        return jnp.concatenate([_bf(z)] * SLAB_HEADS, axis=0) * bd16

    probs = [(bi, ci, si) for bi in range(nb) for ci in range(n_sub) for si in range(n_slabs)]
    pre = {}
    for bi in range(nb):
        for ci in range(n_sub):
            rows = slice((bi * n_sub + ci) * c, (bi * n_sub + ci + 1) * c)
            lw_c, cum_c = log_w[rows], cum[rows]
            total = jnp.sum(lw_c, axis=0, keepdims=True)
            grow = jnp.exp(-cum_c)
            dec_out = jnp.exp(total - cum_c)
            a_t = -kk[rows] * jnp.exp(cum_c - lw_c)
            r_t = r[rows] * jnp.exp(cum_c)
            b_t, k_t = b_vec[rows] * grow, k[rows] * grow
            b_o, k_o = b_vec[rows] * dec_out, k[rows] * dec_out
            w_rows = jnp.broadcast_to(jnp.exp(total), (2 * c, w))
            for si in range(n_slabs):
                ln = slice(si * SLAB, (si + 1) * SLAB)
                ar = _bf(jnp.concatenate([a_t[:, ln], r_t[:, ln]], axis=0))
                g_b = _dot_nt(ar, block_diag(b_t[:, ln]))
                g_k = _dot_nt(ar, block_diag(k_t[:, ln]))
                a_k = jnp.concatenate([jnp.where(strictly_lower, g_k[:c], 0.0),
                                       jnp.where(lower, g_k[c:], 0.0)], axis=0)
                turned = jnp.transpose(jnp.concatenate([b_o[:, ln], k_o[:, ln], w_rows[:, ln]], axis=0))
                pre[bi, ci, si] = dict(
                    ar=ar, v=v[rows, ln],
                    a_ab=jnp.where(strictly_lower, g_b[:c], 0.0),
                    a_rb=_bf(jnp.where(lower, g_b[c:], 0.0)),
                    from_v=_dot(_bf(a_k), block_diag(v[rows, ln])),
                    out_t=_bf(turned[:, :2 * c]),
                    w_col=turned[:, 2 * c:2 * c + 1])

    inv = {p: eye + pre[p]["a_ab"] for p in probs}
    power = {p: _dot(_bf(pre[p]["a_ab"]), block_diag(pre[p]["a_ab"])) for p in probs}
    covered = 2
    while covered * 2 < c:
        for p in probs:
            both = _dot(_bf(jnp.concatenate([inv[p], power[p]], axis=0)), block_diag(power[p]))
            inv[p] = inv[p] + both[:c]
            power[p] = both[c:]
        covered *= 2
    for p in probs:
        inv[p] = _bf(inv[p] + _dot(_bf(inv[p]), block_diag(power[p])))

    states = {(bi, si): state_ref[bi * n_slabs + si] for bi in range(nb) for si in range(n_slabs)}
    y_parts = {}
    chains = list(states)
    for ci in range(n_sub):
        pp = {ch: pre[ch[0], ci, ch[1]] for ch in chains}
        from_state = {ch: _dot(pp[ch]["ar"], _bf(states[ch])) for ch in chains}
        rhs = {ch: block_diag(from_state[ch][:c] + pp[ch]["from_v"][:c]) for ch in chains}
        u = {ch: _dot(inv[ch[0], ci, ch[1]], rhs[ch]) for ch in chains}
        for ch in chains:
            states[ch] = states[ch] * pp[ch]["w_col"] + bd * _dot(
                pp[ch]["out_t"], _bf(jnp.concatenate([u[ch], pp[ch]["v"]], axis=0)))
        for ch in chains:
            y_parts[ch[0], ci, ch[1]] = (from_state[ch][c:] + pp[ch]["from_v"][c:]
                                         + _dot(pp[ch]["a_rb"], block_diag(u[ch])))
    for bi in range(nb):
        for si in range(n_slabs):
            state_ref[bi * n_slabs + si] = states[bi, si]
    y = jnp.concatenate(
        [jnp.concatenate([y_parts[bi, ci, si] for si in range(n_slabs)], axis=1)
         for bi in range(nb) for ci in range(n_sub)], axis=0)

    inv_n = 1.0 / HEAD_DIM
    centered = y - head_sum(y) * inv_n
    var = head_sum(centered * centered) * inv_n
    y = centered * lax.rsqrt(var + GN_EPS) * par(P_LNW) + par(P_LNB)
    o_ref[...] = _bf((y + bonus) * gate).reshape(nb, tl, w)


def _pad_lora(down, up):
    rank = down.shape[1]
    return (_bf(jnp.pad(down, ((0, 0), (0, LORA_PAD - rank)))),
            _bf(jnp.pad(up, ((0, LORA_PAD - rank), (0, 0)))))


def _rwkv_call(rin, chan_params, loras):
    b, s, w4 = rin.shape
    w = w4 // 4
    tl = RWKV_TOKENS
    nb = RWKV_SEQS if b % RWKV_SEQS == 0 else 1
    packed = jnp.zeros((P_ROWS, w), F32).at[:len(chan_params)].set(jnp.stack(chan_params))
    lora_args = []
    for down, up in loras:
        lora_args += list(_pad_lora(down, up))
    assert tl % SLAB == 0
    consts = _rwkv_consts()
    fixed = lambda i, j: (0, 0)
    return pl.pallas_call(
        _rwkv_body,
        grid=(b // nb, s // tl),
        in_specs=[pl.BlockSpec((nb, tl, w4), lambda i, j: (i, j, 0)), pl.BlockSpec((P_ROWS, w), fixed)]
        + [pl.BlockSpec(a.shape, fixed) for a in lora_args]
        + [pl.BlockSpec(a.shape, fixed) for a in consts],
        out_specs=pl.BlockSpec((nb, tl, w), lambda i, j: (i, j, 0)),
        out_shape=jax.ShapeDtypeStruct((b, s, w), BF16),
        scratch_shapes=[pltpu.VMEM((nb * (w // SLAB), SLAB, SLAB), F32), pltpu.VMEM((nb, 8, w4), F32)],
        compiler_params=pltpu.CompilerParams(
            dimension_semantics=("parallel", "arbitrary"), vmem_limit_bytes=V7X_VMEM_LIMIT_BYTES),
        name="rwkv7",
    )(rin, packed, *lora_args, *consts)


def kernel(x, ffn1_norm, ffn1_w_gate, ffn1_w_up, ffn1_w_down, mix_norm, w_in, q_norm, k_norm, mu_r, mu_k, mu_v, mu_w, mu_a, mu_g, w0, w1, w2, a0, a1, a2, g1, g2, k_k, k_a, r_k, ln_x_w, ln_x_b, w_out, ffn2_norm, ffn2_w_gate, ffn2_w_up, ffn2_w_down):
    b, s, d = x.shape
    depth = w_in.shape[0]
    rw_width = mu_r.shape[-1]
    att_width = (w_in.shape[-1] - 4 * rw_width) // 3
    h = x.reshape(b * s, d)
    for l in range(depth):
        h = _ffn_call(h, ffn1_norm[l], _bf(ffn1_w_gate[l]), _bf(ffn1_w_up[l]), _bf(ffn1_w_down[l]))
        outs = _in_proj_call(h.reshape(b, s, d), mix_norm[l], _bf(w_in[l]), q_norm[l], k_norm[l], att_width)
        att = _attn_call(outs[0:3], outs[3:6], outs[6:9])
        rin = outs[9]
        chan = [mu_r[l], mu_k[l], mu_v[l], mu_w[l], mu_a[l], mu_g[l], w0[l], a0[l], k_k[l], k_a[l],
                r_k[l].reshape(rw_width), ln_x_w[l], ln_x_b[l]]
        rw = _rwkv_call(rin, chan, [(w1[l], w2[l]), (a1[l], a2[l]), (g1[l], g2[l])])
        w_o = _bf(w_out[l])
        h = _ffn_call(h, ffn2_norm[l], _bf(ffn2_w_gate[l]), _bf(ffn2_w_up[l]), _bf(ffn2_w_down[l]),
                      mix=(att, rw.reshape(b * s, rw_width), w_o[:att_width], w_o[att_width:]))
    return h.reshape(b, s, d)
```

```python
import functools
import math

import jax
import jax.numpy as jnp
import numpy as np
from jax import lax
from jax.experimental import pallas as pl
from jax.experimental.pallas import tpu as pltpu

F32 = jnp.float32
BF16 = jnp.bfloat16

HEAD_DIM = 64
LANES = 128
DILATED_PATTERNS = ((128, 1), (512, 4), (2048, 16))
ATT_WINDOW = 128
ATT_GROUP = 4
ATT_LOOP_GROUP = 3
FFN_RESIDUAL = 0.5
RMS_EPS = 1e-6
GN_EPS = 64e-5
NEG_INF = -1e30
LORA_PAD = 128

V7X_VMEM_LIMIT_BYTES = 60000 * 1024

FFN_TOKENS = 512
FFN_HIDDEN_CHUNK = 1024
IN_TOKENS = 512
RWKV_TOKENS = 256
RWKV_SEQS = 2
RWKV_CHUNK = 64
SLAB_HEADS = 4
SLAB = SLAB_HEADS * HEAD_DIM


def _bf(x):
    return x.astype(BF16)


def _dot(a, b):
    return jnp.dot(a, b, preferred_element_type=F32)


def _dot_nt(a, b):
    return lax.dot_general(a, b, (((1,), (1,)), ((), ())), preferred_element_type=F32)


def _dot_tn(a, b):
    return lax.dot_general(a, b, (((0,), (0,)), ((), ())), preferred_element_type=F32)


def _split_dot(x, w, parts):
    out = None
    rem = x
    for p in range(parts):
        piece = _bf(rem)
        term = _dot(piece, w)
        out = term if out is None else out + term
        if p + 1 < parts:
            rem = rem - piece.astype(F32)
    return out


def _split_dot_left(w, x, parts):
    out = None
    rem = x
    for p in range(parts):
        piece = _bf(rem)
        term = _dot(w, piece)
        out = term if out is None else out + term
        if p + 1 < parts:
            rem = rem - piece.astype(F32)
    return out


def _head_sums(z, membership, parts):
    g = membership.shape[0]
    return jnp.concatenate([_split_dot(z[:, i:i + g], membership, parts) for i in range(0, z.shape[1], g)],
                           axis=1)


def _rms_norm_rows(x, gain):
    ms = jnp.mean(x * x, axis=-1, keepdims=True)
    return x * lax.rsqrt(ms + RMS_EPS) * gain


def _ffn_body(with_mix, ff_chunks, *refs):
    if with_mix:
        x_ref, att_ref, rw_ref, wo_a_ref, wo_r_ref, nw_ref, wg_ref, wu_ref, wd_ref, o_ref = refs
    else:
        x_ref, nw_ref, wg_ref, wu_ref, wd_ref, o_ref = refs
    x = x_ref[...]
    if with_mix:
        att = jnp.concatenate([att_ref[0, p] for p in range(att_ref.shape[1])], axis=1)
        x = x + _dot(att, wo_a_ref[...]) + _dot(rw_ref[...], wo_r_ref[...])
    h = _bf(_rms_norm_rows(x, nw_ref[...]))
    acc = None
    for lo, hi in ff_chunks:
        g = _dot(h, wg_ref[:, lo:hi])
        u = _dot(h, wu_ref[:, lo:hi])
        part = _dot(_bf(g * jax.nn.sigmoid(g) * u), wd_ref[lo:hi, :])
        acc = part if acc is None else acc + part
    o_ref[...] = x + FFN_RESIDUAL * acc


def _ff_chunks(d_ff):
    bounds = list(range(0, d_ff, FFN_HIDDEN_CHUNK)) + [d_ff]
    return tuple(zip(bounds[:-1], bounds[1:]))


def _ffn_call(x, norm_w, wg, wu, wd, mix=None):
    t, d = x.shape
    tm = FFN_TOKENS
    row = lambda i: (i, 0)
    resident = lambda a: pl.BlockSpec(a.shape, lambda i: (0, 0), pipeline_mode=pl.Buffered(1))
    in_specs = [pl.BlockSpec((tm, d), row)]
    args = [x]
    if mix is not None:
        att, rw, wo_a, wo_r = mix
        _, pairs, s, _ = att.shape
        tiles = s // tm
        in_specs += [pl.BlockSpec((1, pairs, tm, LANES), lambda i: (i // tiles, 0, i % tiles, 0)),
                     pl.BlockSpec((tm, rw.shape[1]), row), resident(wo_a), resident(wo_r)]
        args += [att, rw, wo_a, wo_r]
    norm_w = norm_w.reshape(1, d)
    in_specs += [resident(norm_w), resident(wg), resident(wu), resident(wd)]
    args += [norm_w, wg, wu, wd]
    return pl.pallas_call(
        functools.partial(_ffn_body, mix is not None, _ff_chunks(wg.shape[1])),
        grid=(t // tm,),
        in_specs=in_specs,
        out_specs=pl.BlockSpec((tm, d), row),
        out_shape=jax.ShapeDtypeStruct((t, d), F32),
        compiler_params=pltpu.CompilerParams(
            dimension_semantics=("parallel",), vmem_limit_bytes=V7X_VMEM_LIMIT_BYTES),
        name="ffn_mix" if mix is not None else "ffn",
    )(*args)


def _in_proj_body(att_width, x_ref, nw_ref, w_ref, qg_ref, kg_ref, hs_ref, *out_refs):
    view_refs, rin_ref, stage_refs = out_refs[:9], out_refs[9], out_refs[10:]
    h = _bf(_rms_norm_rows(x_ref[0], nw_ref[...]))
    aw = att_width
    proj = _dot(h, w_ref[:, :3 * aw])
    hs = hs_ref[...]
    tm = proj.shape[0]

    def head_rms(z, gain):
        ms = _head_sums(z * z, hs, 2) * (1.0 / HEAD_DIM)
        return z * lax.rsqrt(ms + RMS_EPS) * gain

    tensors = (head_rms(proj[:, 0:aw], qg_ref[...]) * (HEAD_DIM ** -0.5),
               head_rms(proj[:, aw:2 * aw], kg_ref[...]),
               proj[:, 2 * aw:3 * aw])
    rw_cols = rin_ref.shape[-1]
    rw_step = rw_cols // (len(tensors) + 1)

    def rw_part(i):
        cols = slice(i * rw_step, (i + 1) * rw_step)
        rin_ref[0, :, cols] = _dot(h, w_ref[:, 3 * aw + cols.start:3 * aw + cols.stop])

    pairs = aw // LANES
    dils = [dil for _, dil in DILATED_PATTERNS]
    assert dils[0] == 1 and len(stage_refs) == len(dils) - 1
    rw_part(len(tensors))
    for ti, val in enumerate(tensors):
        rw_part(ti)
        for p in range(pairs):
            nat = val[:, p * LANES:(p + 1) * LANES]
            stage_refs[0][p] = nat
            view_refs[3 * ti][0, p] = _bf(nat)
        for li in range(1, len(dils)):
            prev_dil, dil = dils[li - 1], dils[li]
            ratio = dil // prev_dil
            for p in range(pairs):
                for cp in range(prev_dil):
                    for c2 in range(ratio):
                        c = cp + prev_dil * c2
                        blk = stage_refs[li - 1][p * prev_dil + cp, pl.ds(c2, tm // dil, stride=ratio), :]
                        if li + 1 < len(dils):
                            stage_refs[li][p * dil + c] = blk
                        view_refs[3 * ti + li][0, p, :, c * LANES:(c + 1) * LANES] = _bf(blk)


def _head_sum_matrix(width):
    head = np.arange(width) // HEAD_DIM
    return jnp.asarray(head[:, None] == head[None, :], dtype=BF16)


def _in_proj_call(x, norm_w, w_in, q_gain, k_gain, att_width):
    b, s, d = x.shape
    cols = w_in.shape[1]
    rw_cols = cols - 3 * att_width
    tm = IN_TOKENS
    heads = att_width // HEAD_DIM
    pairs = att_width // LANES
    fixed = lambda i, j: (0, 0)
    view_shapes, view_specs = [], []
    for _, dil in DILATED_PATTERNS:
        view_shapes.append(jax.ShapeDtypeStruct((b, pairs, s // dil, dil * LANES), BF16))
        view_specs.append(pl.BlockSpec((1, pairs, tm // dil, dil * LANES), lambda i, j: (i, 0, j, 0)))
    return pl.pallas_call(
        functools.partial(_in_proj_body, att_width),
        grid=(b, s // tm),
        in_specs=[pl.BlockSpec((1, tm, d), lambda i, j: (i, j, 0)), pl.BlockSpec((1, d), fixed),
                  pl.BlockSpec((d, cols), fixed, pipeline_mode=pl.Buffered(1)),
                  pl.BlockSpec((1, att_width), fixed), pl.BlockSpec((1, att_width), fixed),
                  pl.BlockSpec((SLAB, SLAB), fixed)],
        out_specs=view_specs * 3 + [pl.BlockSpec((1, tm, rw_cols), lambda i, j: (i, j, 0))],
        out_shape=view_shapes * 3 + [jax.ShapeDtypeStruct((b, s, rw_cols), F32)],
        scratch_shapes=[pltpu.VMEM((pairs * dil, tm // dil, LANES), F32) for _, dil in DILATED_PATTERNS[:-1]],
        compiler_params=pltpu.CompilerParams(
            dimension_semantics=("parallel", "parallel"), vmem_limit_bytes=V7X_VMEM_LIMIT_BYTES),
        name="in_proj",
    )(x, norm_w.reshape(1, d), w_in, jnp.tile(q_gain, heads).reshape(1, att_width),
      jnp.tile(k_gain, heads).reshape(1, att_width), _head_sum_matrix(SLAB))


def _attn_masks():
    w = ATT_WINDOW
    r = np.arange(w)[:, None]
    causal = np.arange(w)[None, :] <= r
    u = np.arange(2 * w)[None, :]
    band = (u >= r) & (u <= r + w)
    two_heads = lambda m: jnp.asarray(np.concatenate([m, m], axis=0), F32)
    return two_heads(causal), two_heads(band)


def _attn_body(seq, qn_ref, kn_ref, vn_ref, q4_ref, k4_ref, v4_ref, q16_ref, k16_ref, v16_ref,
               causal_ref, band_ref, o_ref, *part_refs):
    w = ATT_WINDOW
    lower_head = lax.broadcasted_iota(jnp.int32, (1, LANES), 1) < HEAD_DIM
    causal = causal_ref[...] > 0.0
    band = band_ref[...] > 0.0

    def pick(two):
        return jnp.where(lower_head, two[:w], two[w:])

    def blocks(tasks):
        qq = []
        for q, _, _, _ in tasks:
            zero = jnp.zeros_like(q)
            qq.append(jnp.concatenate([jnp.where(lower_head, q, zero), jnp.where(lower_head, zero, q)], axis=0))
        s = [jnp.where(t[3], _dot_nt(a, t[1]), NEG_INF) for a, t in zip(qq, tasks)]
        m = [jnp.max(x, axis=1, keepdims=True) for x in s]
        p = [_bf(jnp.exp(x - mx)) for x, mx in zip(s, m)]
        ol = [_dot(x, jnp.concatenate([t[2], jnp.ones_like(t[2])], axis=1)) for x, t in zip(p, tasks)]
        results = []
        for oli, mi in zip(ol, m):
            l_full = pick(oli[:, LANES:])
            m_full = pick(jnp.broadcast_to(mi, (2 * w, LANES)))
            results.append((pick(oli[:, :LANES]) / l_full, m_full + jnp.log(l_full)))
        return results

    def run_pattern(q_ref, k_ref, v_ref, dil, out_ref, lse_ref):
        n_blocks = seq // dil // w

        def offset(nb, size):
            return nb * size if isinstance(nb, int) else pl.multiple_of(nb * size, size)

        def task(c, nb):
            lanes = slice(c * LANES, (c + 1) * LANES)
            if isinstance(nb, int) and nb == 0:
                rows = slice(0, w)
                return q_ref[0, 0, rows, lanes], k_ref[0, 0, rows, lanes], v_ref[0, 0, rows, lanes], causal
            q_rows = pl.ds(offset(nb, w), w)
            k_rows = pl.ds(offset(nb - 1, w), 2 * w)
            return q_ref[0, 0, q_rows, lanes], k_ref[0, 0, k_rows, lanes], v_ref[0, 0, k_rows, lanes], band

        def store(c, nb, result):
            rows = pl.ds(nb * (w * dil) + c, w, stride=dil) if dil > 1 else pl.ds(offset(nb, w), w)
            out_ref[rows, :], lse_ref[rows, :] = result

        def run(group):
            for (c, nb), result in zip(group, blocks([task(c, nb) for c, nb in group])):
                store(c, nb, result)

        looped = n_blocks > ATT_GROUP
        static = [(c, nb) for c in range(dil) for nb in range(1 if looped else n_blocks)]
        for g in range(0, len(static), ATT_GROUP):
            run(static[g:g + ATT_GROUP])
        if looped:
            assert (n_blocks - 1) % ATT_LOOP_GROUP == 0
            for c in range(dil):
                def trip(i, carry):
                    run([(c, 1 + i * ATT_LOOP_GROUP + t) for t in range(ATT_LOOP_GROUP)])
                    return carry

                lax.fori_loop(0, (n_blocks - 1) // ATT_LOOP_GROUP, trip, 0)

    views = ((qn_ref, kn_ref, vn_ref), (q4_ref, k4_ref, v4_ref), (q16_ref, k16_ref, v16_ref))
    for (window, dil), (q_ref, k_ref, v_ref), pi in zip(DILATED_PATTERNS, views, range(3)):
        assert window // dil == w
        run_pattern(q_ref, k_ref, v_ref, dil, part_refs[2 * pi], part_refs[2 * pi + 1])

    def merge(i, carry):
        rows = pl.ds(pl.multiple_of(i * w, w), w)
        outs = [part_refs[2 * pi][rows, :] for pi in range(3)]
        lses = [part_refs[2 * pi + 1][rows, :] for pi in range(3)]
        top = jnp.maximum(jnp.maximum(lses[0], lses[1]), lses[2])
        wts = [jnp.exp(l - top) for l in lses]
        o_ref[0, 0, rows, :] = _bf((wts[0] * outs[0] + wts[1] * outs[1] + wts[2] * outs[2])
                                   / (wts[0] + wts[1] + wts[2]))
        return carry

    lax.fori_loop(0, seq // w, merge, 0)


def _attn_call(q_views, k_views, v_views):
    b, pairs, s, _ = q_views[0].shape
    in_specs, args = [], []
    for pi in range(len(DILATED_PATTERNS)):
        for views in (q_views, k_views, v_views):
            in_specs.append(pl.BlockSpec((1, 1) + views[pi].shape[2:], lambda i, j: (i, j, 0, 0)))
            args.append(views[pi])
    masks = _attn_masks()
    in_specs += [pl.BlockSpec(m.shape, lambda i, j: (0, 0)) for m in masks]
    return pl.pallas_call(
        functools.partial(_attn_body, s),
        grid=(b, pairs),
        in_specs=in_specs,
        out_specs=pl.BlockSpec((1, 1, s, LANES), lambda i, j: (i, j, 0, 0)),
        out_shape=jax.ShapeDtypeStruct((b, pairs, s, LANES), BF16),
        scratch_shapes=[pltpu.VMEM((s, LANES), F32)] * 6,
        compiler_params=pltpu.CompilerParams(
            dimension_semantics=("parallel", "parallel"), vmem_limit_bytes=V7X_VMEM_LIMIT_BYTES),
        name="dilated_attn",
    )(*args, *masks)


(P_MU_R, P_MU_K, P_MU_V, P_MU_W, P_MU_A, P_MU_G, P_W0, P_A0, P_KK, P_KA, P_RK, P_LNW, P_LNB) = range(13)
P_ROWS = 16


def _rwkv_consts():
    c = RWKV_CHUNK
    assert c == HEAD_DIM
    t = np.arange(SLAB)
    tri = ((t[:, None] // c == t[None, :] // c) & (t[None, :] <= t[:, None]))
    blockdiag = (np.arange(SLAB)[:, None] // HEAD_DIM == np.arange(SLAB)[None, :] // HEAD_DIM)
    src = np.arange(SLAB_HEADS * c) % c
    tok = np.arange(c)[:, None]
    return (jnp.asarray(tri, BF16),
            jnp.asarray(blockdiag, F32),
            jnp.asarray(blockdiag, BF16),
            jnp.asarray(src[None, :] < tok, F32),
            jnp.asarray(src[None, :] <= tok, F32),
            jnp.asarray(src[None, :] == tok, F32))


def _rwkv_body(rin_ref, p_ref, w1_ref, w2_ref, a1_ref, a2_ref, g1_ref, g2_ref,
               tri_ref, bd_ref, bd16_ref, lt_ref, le_ref, eye_ref, o_ref, state_ref, prev_ref):
    c = RWKV_CHUNK
    nb, tl, w4 = rin_ref.shape
    w = o_ref.shape[-1]
    n_slabs = w // SLAB
    n_sub = tl // c
    step = pl.program_id(1)

    @pl.when(step == 0)
    def _():
        state_ref[...] = jnp.zeros_like(state_ref)
        prev_ref[...] = jnp.zeros_like(prev_ref)

    x = rin_ref[...].reshape(nb * tl, w4)
    row = lax.broadcasted_iota(jnp.int32, (nb * tl, 1), 0)
    x_prev = pltpu.roll(x, 1, 0)
    for bi in range(nb):
        x_prev = jnp.where(row == bi * tl, prev_ref[bi, 0:1, :], x_prev)
    for bi in range(nb):
        prev_ref[bi, 0:1, :] = rin_ref[bi, tl - 1:tl, :]

    def par(prow):
        return p_ref[prow:prow + 1, :]

    def sect(a, i):
        return a[:, i * w:(i + 1) * w]

    def lerp(i, mu_row):
        cur, prev = sect(x, i), sect(x_prev, i)
        return cur + (prev - cur) * par(mu_row)

    r = lerp(0, P_MU_R)
    k = lerp(1, P_MU_K)
    v = lerp(2, P_MU_V)
    cw = lerp(3, P_MU_W)
    ca = lerp(3, P_MU_A)
    cg = lerp(3, P_MU_G)

    bd = bd_ref[...]
    bd16 = bd16_ref[...]

    def head_sum(z):
        return _head_sums(z, bd16, 2)

    zw = par(P_W0) + _dot(_bf(jnp.tanh(_dot(_bf(cw), w1_ref[...]))), w2_ref[...])
    log_w = -math.exp(-0.5) * jax.nn.sigmoid(zw)
    a_gate = jax.nn.sigmoid(par(P_A0) + _dot(_bf(_dot(_bf(ca), a1_ref[...])), a2_ref[...]))
    gate = _dot(_bf(jax.nn.sigmoid(_dot(_bf(cg), g1_ref[...]))), g2_ref[...])

    kk = k * par(P_KK)
    kk = kk / jnp.maximum(jnp.sqrt(head_sum(kk * kk)), 1e-12)
    k = k * (1.0 + (a_gate - 1.0) * par(P_KA))
    b_vec = kk * a_gate
    bonus = head_sum(r * k * par(P_RK)) * v

    tri = tri_ref[...]
    cum = jnp.concatenate([_split_dot_left(tri, log_w[i:i + SLAB], 3) for i in range(0, nb * tl, SLAB)], axis=0)
    strictly_lower = lt_ref[...] > 0.0
    lower = le_ref[...] > 0.0
    eye = eye_ref[...]

    def block_diag(z):
        return jnp.concatenate([_bf(z)] * SLAB_HEADS, axis=0) * bd16

    probs = [(bi, ci, si) for bi in range(nb) for ci in range(n_sub) for si in range(n_slabs)]
    pre = {}
    for bi in range(nb):
        for ci in range(n_sub):
            rows = slice((bi * n_sub + ci) * c, (bi * n_sub + ci + 1) * c)
            lw_c, cum_c = log_w[rows], cum[rows]
            total = jnp.sum(lw_c, axis=0, keepdims=True)
            grow = jnp.exp(-cum_c)
            dec_out = jnp.exp(total - cum_c)
            a_t = -kk[rows] * jnp.exp(cum_c - lw_c)
            r_t = r[rows] * jnp.exp(cum_c)
            b_t, k_t = b_vec[rows] * grow, k[rows] * grow
            b_o, k_o = b_vec[rows] * dec_out, k[rows] * dec_out
            w_rows = jnp.broadcast_to(jnp.exp(total), (2 * c, w))
            for si in range(n_slabs):
                ln = slice(si * SLAB, (si + 1) * SLAB)
                ar = _bf(jnp.concatenate([a_t[:, ln], r_t[:, ln]], axis=0))
                g_b = _dot_nt(ar, block_diag(b_t[:, ln]))
                g_k = _dot_nt(ar, block_diag(k_t[:, ln]))
                a_k = jnp.concatenate([jnp.where(strictly_lower, g_k[:c], 0.0),
                                       jnp.where(lower, g_k[c:], 0.0)], axis=0)
                turned = jnp.transpose(jnp.concatenate([b_o[:, ln], k_o[:, ln], w_rows[:, ln]], axis=0))
                pre[bi, ci, si] = dict(
                    ar=ar, v=v[rows, ln],
                    a_ab=jnp.where(strictly_lower, g_b[:c], 0.0),
                    a_rb=_bf(jnp.where(lower, g_b[c:], 0.0)),
                    from_v=_dot(_bf(a_k), block_diag(v[rows, ln])),
                    out_t=_bf(turned[:, :2 * c]),
                    w_col=turned[:, 2 * c:2 * c + 1])

    inv = {p: eye + pre[p]["a_ab"] for p in probs}
    power = {p: _dot(_bf(pre[p]["a_ab"]), block_diag(pre[p]["a_ab"])) for p in probs}
    covered = 2
    while covered * 2 < c:
        for p in probs:
            both = _dot(_bf(jnp.concatenate([inv[p], power[p]], axis=0)), block_diag(power[p]))
            inv[p] = inv[p] + both[:c]
            power[p] = both[c:]
        covered *= 2
    for p in probs:
        inv[p] = _bf(inv[p] + _dot(_bf(inv[p]), block_diag(power[p])))

    states = {(bi, si): state_ref[bi * n_slabs + si] for bi in range(nb) for si in range(n_slabs)}
    y_parts = {}
    chains = list(states)
    for ci in range(n_sub):
        pp = {ch: pre[ch[0], ci, ch[1]] for ch in chains}
        from_state = {ch: _dot(pp[ch]["ar"], _bf(states[ch])) for ch in chains}
        rhs = {ch: block_diag(from_state[ch][:c] + pp[ch]["from_v"][:c]) for ch in chains}
        u = {ch: _dot(inv[ch[0], ci, ch[1]], rhs[ch]) for ch in chains}
        for ch in chains:
            states[ch] = states[ch] * pp[ch]["w_col"] + bd * _dot(
                pp[ch]["out_t"], _bf(jnp.concatenate([u[ch], pp[ch]["v"]], axis=0)))
        for ch in chains:
            y_parts[ch[0], ci, ch[1]] = (from_state[ch][c:] + pp[ch]["from_v"][c:]
                                         + _dot(pp[ch]["a_rb"], block_diag(u[ch])))
    for bi in range(nb):
        for si in range(n_slabs):
            state_ref[bi * n_slabs + si] = states[bi, si]
    y = jnp.concatenate(
        [jnp.concatenate([y_parts[bi, ci, si] for si in range(n_slabs)], axis=1)
         for bi in range(nb) for ci in range(n_sub)], axis=0)

    inv_n = 1.0 / HEAD_DIM
    centered = y - head_sum(y) * inv_n
    var = head_sum(centered * centered) * inv_n
    y = centered * lax.rsqrt(var + GN_EPS) * par(P_LNW) + par(P_LNB)
    o_ref[...] = _bf((y + bonus) * gate).reshape(nb, tl, w)


def _pad_lora(down, up):
    rank = down.shape[1]
    return (_bf(jnp.pad(down, ((0, 0), (0, LORA_PAD - rank)))),
            _bf(jnp.pad(up, ((0, LORA_PAD - rank), (0, 0)))))


def _rwkv_call(rin, chan_params, loras):
    b, s, w4 = rin.shape
    w = w4 // 4
    tl = RWKV_TOKENS
    nb = RWKV_SEQS if b % RWKV_SEQS == 0 else 1
    packed = jnp.zeros((P_ROWS, w), F32).at[:len(chan_params)].set(jnp.stack(chan_params))
    lora_args = []
    for down, up in loras:
        lora_args += list(_pad_lora(down, up))
    assert tl % SLAB == 0
    consts = _rwkv_consts()
    fixed = lambda i, j: (0, 0)
    return pl.pallas_call(
        _rwkv_body,
        grid=(b // nb, s // tl),
        in_specs=[pl.BlockSpec((nb, tl, w4), lambda i, j: (i, j, 0)), pl.BlockSpec((P_ROWS, w), fixed)]
        + [pl.BlockSpec(a.shape, fixed) for a in lora_args]
        + [pl.BlockSpec(a.shape, fixed) for a in consts],
        out_specs=pl.BlockSpec((nb, tl, w), lambda i, j: (i, j, 0)),
        out_shape=jax.ShapeDtypeStruct((b, s, w), BF16),
        scratch_shapes=[pltpu.VMEM((nb * (w // SLAB), SLAB, SLAB), F32), pltpu.VMEM((nb, 8, w4), F32)],
        compiler_params=pltpu.CompilerParams(
            dimension_semantics=("parallel", "arbitrary"), vmem_limit_bytes=V7X_VMEM_LIMIT_BYTES),
        name="rwkv7",
    )(rin, packed, *lora_args, *consts)


def kernel(x, ffn1_norm, ffn1_w_gate, ffn1_w_up, ffn1_w_down, mix_norm, w_in, q_norm, k_norm, mu_r, mu_k, mu_v, mu_w, mu_a, mu_g, w0, w1, w2, a0, a1, a2, g1, g2, k_k, k_a, r_k, ln_x_w, ln_x_b, w_out, ffn2_norm, ffn2_w_gate, ffn2_w_up, ffn2_w_down):
    b, s, d = x.shape
    depth = w_in.shape[0]
    rw_width = mu_r.shape[-1]
    att_width = (w_in.shape[-1] - 4 * rw_width) // 3
    h = x.reshape(b * s, d)
    for l in range(depth):
        h = _ffn_call(h, ffn1_norm[l], _bf(ffn1_w_gate[l]), _bf(ffn1_w_up[l]), _bf(ffn1_w_down[l]))
        outs = _in_proj_call(h.reshape(b, s, d), mix_norm[l], _bf(w_in[l]), q_norm[l], k_norm[l], att_width)
        att = _attn_call(outs[0:3], outs[3:6], outs[6:9])
        rin = outs[9]
        chan = [mu_r[l], mu_k[l], mu_v[l], mu_w[l], mu_a[l], mu_g[l], w0[l], a0[l], k_k[l], k_a[l],
                r_k[l].reshape(rw_width), ln_x_w[l], ln_x_b[l]]
        rw = _rwkv_call(rin, chan, [(w1[l], w2[l]), (a1[l], a2[l]), (g1[l], g2[l])])
        w_o = _bf(w_out[l])
        h = _ffn_call(h, ffn2_norm[l], _bf(ffn2_w_gate[l]), _bf(ffn2_w_up[l]), _bf(ffn2_w_down[l]),
                      mix=(att, rw.reshape(b * s, rw_width), w_o[:att_width], w_o[att_width:]))
    return h.reshape(b, s, d)
```

```python
import functools
import math

import jax
import jax.numpy as jnp
import numpy as np
from jax import lax
from jax.experimental import pallas as pl
from jax.experimental.pallas import tpu as pltpu

F32 = jnp.float32
BF16 = jnp.bfloat16

HEAD_DIM = 64
LANES = 128
DILATED_PATTERNS = ((128, 1), (512, 4), (2048, 16))
ATT_WINDOW = 128
ATT_GROUP = 4
ATT_LOOP_GROUP = 3
FFN_RESIDUAL = 0.5
RMS_EPS = 1e-6
GN_EPS = 64e-5
NEG_INF = -1e30
LORA_PAD = 128

V7X_VMEM_LIMIT_BYTES = 60000 * 1024

FFN_TOKENS = 512
FFN_HIDDEN_CHUNK = 1024
WEIGHT_LOAD_ROWS = 256
IN_TOKENS = 512
RWKV_TOKENS = 256
RWKV_SEQS = 2
RWKV_CHUNK = 64
SLAB_HEADS = 4
SLAB = SLAB_HEADS * HEAD_DIM


def _bf(x):
    return x.astype(BF16)


def _dot(a, b):
    return jnp.dot(a, b, preferred_element_type=F32)


def _dot_nt(a, b):
    return lax.dot_general(a, b, (((1,), (1,)), ((), ())), preferred_element_type=F32)


def _dot_tn(a, b):
    return lax.dot_general(a, b, (((0,), (0,)), ((), ())), preferred_element_type=F32)


def _split_dot(x, w, parts):
    out = None
    rem = x
    for p in range(parts):
        piece = _bf(rem)
        term = _dot(piece, w)
        out = term if out is None else out + term
        if p + 1 < parts:
            rem = rem - piece.astype(F32)
    return out


def _split_dot_left(w, x, parts):
    out = None
    rem = x
    for p in range(parts):
        piece = _bf(rem)
        term = _dot(w, piece)
        out = term if out is None else out + term
        if p + 1 < parts:
            rem = rem - piece.astype(F32)
    return out


def _head_sums(z, membership, parts):
    g = membership.shape[0]
    return jnp.concatenate([_split_dot(z[:, i:i + g], membership, parts) for i in range(0, z.shape[1], g)],
                           axis=1)


def _rms_norm_rows(x, gain):
    ms = jnp.mean(x * x, axis=-1, keepdims=True)
    return x * lax.rsqrt(ms + RMS_EPS) * gain


def _load_as_bf16(src_hbm, dst_ref, stage_ref, sem_ref):
    n_rows, cols = src_hbm.shape
    rows = stage_ref.shape[1]
    assert n_rows % rows == 0
    n = n_rows // rows

    def copy(i):
        slot = i % 2
        return pltpu.make_async_copy(src_hbm.at[pl.ds(i * rows, rows), :],
                                     stage_ref.at[slot, :, pl.ds(0, cols)], sem_ref.at[slot])

    copy(0).start()
    for i in range(n):
        if i + 1 < n:
            copy(i + 1).start()
        copy(i).wait()
        dst_ref[pl.ds(i * rows, rows), :] = _bf(stage_ref[i % 2, :, pl.ds(0, cols)])


def _ffn_body(with_mix, ff_chunks, layer, *refs):
    if with_mix:
        (x_ref, att_ref, rw_ref, nw_ref, wo_hbm, wg_hbm, wu_hbm, wd_hbm, o_ref,
         wo_ref, wg_ref, wu_ref, wd_ref, stage_ref, sem_ref) = refs
    else:
        x_ref, nw_ref, wg_hbm, wu_hbm, wd_hbm, o_ref, wg_ref, wu_ref, wd_ref, stage_ref, sem_ref = refs

    @pl.when(pl.program_id(0) == 0)
    def _():
        if with_mix:
            _load_as_bf16(wo_hbm.at[layer], wo_ref, stage_ref, sem_ref)
        _load_as_bf16(wg_hbm.at[layer], wg_ref, stage_ref, sem_ref)
        _load_as_bf16(wu_hbm.at[layer], wu_ref, stage_ref, sem_ref)
        _load_as_bf16(wd_hbm.at[layer], wd_ref, stage_ref, sem_ref)

    x = x_ref[...]
    if with_mix:
        mixed = jnp.concatenate([att_ref[0, p] for p in range(att_ref.shape[1])] + [rw_ref[...]], axis=1)
        x = x + _dot(mixed, wo_ref[...])
    h = _bf(_rms_norm_rows(x, nw_ref[...]))
    acc = None
    for lo, hi in ff_chunks:
        g = _dot(h, wg_ref[:, lo:hi])
        u = _dot(h, wu_ref[:, lo:hi])
        part = _dot(_bf(g * jax.nn.sigmoid(g) * u), wd_ref[lo:hi, :])
        acc = part if acc is None else acc + part
    o_ref[...] = x + FFN_RESIDUAL * acc


def _ff_chunks(d_ff):
    bounds = list(range(0, d_ff, FFN_HIDDEN_CHUNK)) + [d_ff]
    return tuple(zip(bounds[:-1], bounds[1:]))


def _ffn_call(x, norm_w, wg, wu, wd, layer, mix=None):
    t, d = x.shape
    d_ff = wg.shape[2]
    tm = FFN_TOKENS
    row = lambda i: (i, 0)
    hbm = pl.BlockSpec(memory_space=pl.ANY)
    in_specs = [pl.BlockSpec((tm, d), row)]
    args = [x]
    scratch = []
    if mix is not None:
        att, rw, w_out = mix
        _, pairs, s, _ = att.shape
        tiles = s // tm
        in_specs += [pl.BlockSpec((1, pairs, tm, LANES), lambda i: (i // tiles, 0, i % tiles, 0)),
                     pl.BlockSpec((tm, rw.shape[1]), row)]
        args += [att, rw]
    in_specs.append(pl.BlockSpec((1, d), lambda i: (0, 0)))
    args.append(norm_w.reshape(1, d))
    if mix is not None:
        in_specs.append(hbm)
        args.append(w_out)
        scratch.append(pltpu.VMEM(w_out.shape[1:], BF16))
    in_specs += [hbm, hbm, hbm]
    args += [wg, wu, wd]
    scratch += [pltpu.VMEM((d, d_ff), BF16), pltpu.VMEM((d, d_ff), BF16), pltpu.VMEM((d_ff, d), BF16),
                pltpu.VMEM((2, WEIGHT_LOAD_ROWS, max(d, d_ff)), F32), pltpu.SemaphoreType.DMA((2,))]
    return pl.pallas_call(
        functools.partial(_ffn_body, mix is not None, _ff_chunks(d_ff), layer),
        grid=(t // tm,),
        in_specs=in_specs,
        out_specs=pl.BlockSpec((tm, d), row),
        out_shape=jax.ShapeDtypeStruct((t, d), F32),
        scratch_shapes=scratch,
        compiler_params=pltpu.CompilerParams(
            dimension_semantics=("arbitrary",), vmem_limit_bytes=V7X_VMEM_LIMIT_BYTES),
        name="ffn_mix" if mix is not None else "ffn",
    )(*args)


def _in_proj_body(att_width, x_ref, nw_ref, w_ref, qg_ref, kg_ref, hs_ref, *out_refs):
    view_refs, rin_ref, stage_refs = out_refs[:9], out_refs[9], out_refs[10:]
    h = _bf(_rms_norm_rows(x_ref[0], nw_ref[...]))
    aw = att_width
    proj = _dot(h, w_ref[:, :3 * aw])
    hs = hs_ref[...]
    tm = proj.shape[0]

    def head_rms(z, gain):
        ms = _head_sums(z * z, hs, 2) * (1.0 / HEAD_DIM)
        return z * lax.rsqrt(ms + RMS_EPS) * gain

    tensors = (head_rms(proj[:, 0:aw], qg_ref[...]) * (HEAD_DIM ** -0.5),
               head_rms(proj[:, aw:2 * aw], kg_ref[...]),
               proj[:, 2 * aw:3 * aw])
    rw_cols = rin_ref.shape[-1]
    rw_step = rw_cols // (len(tensors) + 1)

    def rw_part(i):
        cols = slice(i * rw_step, (i + 1) * rw_step)
        rin_ref[0, :, cols] = _dot(h, w_ref[:, 3 * aw + cols.start:3 * aw + cols.stop])

    pairs = aw // LANES
    dils = [dil for _, dil in DILATED_PATTERNS]
    assert dils[0] == 1 and len(stage_refs) == len(dils) - 1
    rw_part(len(tensors))
    for ti, val in enumerate(tensors):
        rw_part(ti)
        for p in range(pairs):
            nat = val[:, p * LANES:(p + 1) * LANES]
            stage_refs[0][p] = nat
            view_refs[3 * ti][0, p] = _bf(nat)
        for li in range(1, len(dils)):
            prev_dil, dil = dils[li - 1], dils[li]
            ratio = dil // prev_dil
            for p in range(pairs):
                for cp in range(prev_dil):
                    for c2 in range(ratio):
                        c = cp + prev_dil * c2
                        blk = stage_refs[li - 1][p * prev_dil + cp, pl.ds(c2, tm // dil, stride=ratio), :]
                        if li + 1 < len(dils):
                            stage_refs[li][p * dil + c] = blk
                        view_refs[3 * ti + li][0, p, :, c * LANES:(c + 1) * LANES] = _bf(blk)


def _head_sum_matrix(width):
    head = np.arange(width) // HEAD_DIM
    return jnp.asarray(head[:, None] == head[None, :], dtype=BF16)


def _in_proj_call(x, norm_w, w_in, q_gain, k_gain, att_width):
    b, s, d = x.shape
    cols = w_in.shape[1]
    rw_cols = cols - 3 * att_width
    tm = IN_TOKENS
    heads = att_width // HEAD_DIM
    pairs = att_width // LANES
    fixed = lambda i, j: (0, 0)
    view_shapes, view_specs = [], []
    for _, dil in DILATED_PATTERNS:
        view_shapes.append(jax.ShapeDtypeStruct((b, pairs, s // dil, dil * LANES), BF16))
        view_specs.append(pl.BlockSpec((1, pairs, tm // dil, dil * LANES), lambda i, j: (i, 0, j, 0)))
    return pl.pallas_call(
        functools.partial(_in_proj_body, att_width),
        grid=(b, s // tm),
        in_specs=[pl.BlockSpec((1, tm, d), lambda i, j: (i, j, 0)), pl.BlockSpec((1, d), fixed),
                  pl.BlockSpec((d, cols), fixed, pipeline_mode=pl.Buffered(1)),
                  pl.BlockSpec((1, att_width), fixed), pl.BlockSpec((1, att_width), fixed),
                  pl.BlockSpec((SLAB, SLAB), fixed)],
        out_specs=view_specs * 3 + [pl.BlockSpec((1, tm, rw_cols), lambda i, j: (i, j, 0))],
        out_shape=view_shapes * 3 + [jax.ShapeDtypeStruct((b, s, rw_cols), F32)],
        scratch_shapes=[pltpu.VMEM((pairs * dil, tm // dil, LANES), F32) for _, dil in DILATED_PATTERNS[:-1]],
        compiler_params=pltpu.CompilerParams(
            dimension_semantics=("parallel", "parallel"), vmem_limit_bytes=V7X_VMEM_LIMIT_BYTES),
        name="in_proj",
    )(x, norm_w.reshape(1, d), w_in, jnp.tile(q_gain, heads).reshape(1, att_width),
      jnp.tile(k_gain, heads).reshape(1, att_width), _head_sum_matrix(SLAB))


def _attn_masks():
    w = ATT_WINDOW
    r = np.arange(w)[:, None]
    causal = np.arange(w)[None, :] <= r
    u = np.arange(2 * w)[None, :]
    band = (u >= r) & (u <= r + w)
    two_heads = lambda m: jnp.asarray(np.concatenate([m, m], axis=0), F32)
    return two_heads(causal), two_heads(band)


def _attn_body(seq, qn_ref, kn_ref, vn_ref, q4_ref, k4_ref, v4_ref, q16_ref, k16_ref, v16_ref,
               causal_ref, band_ref, o_ref, *part_refs):
    w = ATT_WINDOW
    lower_head = lax.broadcasted_iota(jnp.int32, (1, LANES), 1) < HEAD_DIM
    causal = causal_ref[...] > 0.0
    band = band_ref[...] > 0.0

    def pick(two):
        return jnp.where(lower_head, two[:w], two[w:])

    def blocks(tasks):
        qq = []
        for q, _, _, _ in tasks:
            zero = jnp.zeros_like(q)
            qq.append(jnp.concatenate([jnp.where(lower_head, q, zero), jnp.where(lower_head, zero, q)], axis=0))
        s = [jnp.where(t[3], _dot_nt(a, t[1]), NEG_INF) for a, t in zip(qq, tasks)]
        m = [jnp.max(x, axis=1, keepdims=True) for x in s]
        p = [_bf(jnp.exp(x - mx)) for x, mx in zip(s, m)]
        ol = [_dot(x, jnp.concatenate([t[2], jnp.ones_like(t[2])], axis=1)) for x, t in zip(p, tasks)]
        results = []
        for oli, mi in zip(ol, m):
            l_full = pick(oli[:, LANES:])
            m_full = pick(jnp.broadcast_to(mi, (2 * w, LANES)))
            results.append((pick(oli[:, :LANES]) / l_full, m_full + jnp.log(l_full)))
        return results

    def run_pattern(q_ref, k_ref, v_ref, dil, out_ref, lse_ref):
        n_blocks = seq // dil // w

        def offset(nb, size):
            return nb * size if isinstance(nb, int) else pl.multiple_of(nb * size, size)

        def task(c, nb):
            lanes = slice(c * LANES, (c + 1) * LANES)
            if isinstance(nb, int) and nb == 0:
                rows = slice(0, w)
                return q_ref[0, 0, rows, lanes], k_ref[0, 0, rows, lanes], v_ref[0, 0, rows, lanes], causal
            q_rows = pl.ds(offset(nb, w), w)
            k_rows = pl.ds(offset(nb - 1, w), 2 * w)
            return q_ref[0, 0, q_rows, lanes], k_ref[0, 0, k_rows, lanes], v_ref[0, 0, k_rows, lanes], band

        def store(c, nb, result):
            rows = pl.ds(nb * (w * dil) + c, w, stride=dil) if dil > 1 else pl.ds(offset(nb, w), w)
            out_ref[rows, :], lse_ref[rows, :] = result

        def run(group):
            for (c, nb), result in zip(group, blocks([task(c, nb) for c, nb in group])):
                store(c, nb, result)

        looped = n_blocks > ATT_GROUP
        static = [(c, nb) for c in range(dil) for nb in range(1 if looped else n_blocks)]
        for g in range(0, len(static), ATT_GROUP):
            run(static[g:g + ATT_GROUP])
        if looped:
            assert (n_blocks - 1) % ATT_LOOP_GROUP == 0
            for c in range(dil):
                def trip(i, carry):
                    run([(c, 1 + i * ATT_LOOP_GROUP + t) for t in range(ATT_LOOP_GROUP)])
                    return carry

                lax.fori_loop(0, (n_blocks - 1) // ATT_LOOP_GROUP, trip, 0)

    views = ((qn_ref, kn_ref, vn_ref), (q4_ref, k4_ref, v4_ref), (q16_ref, k16_ref, v16_ref))
    for (window, dil), (q_ref, k_ref, v_ref), pi in zip(DILATED_PATTERNS, views, range(3)):
        assert window // dil == w
        run_pattern(q_ref, k_ref, v_ref, dil, part_refs[2 * pi], part_refs[2 * pi + 1])

    def merge(i, carry):
        rows = pl.ds(pl.multiple_of(i * w, w), w)
        outs = [part_refs[2 * pi][rows, :] for pi in range(3)]
        lses = [part_refs[2 * pi + 1][rows, :] for pi in range(3)]
        top = jnp.maximum(jnp.maximum(lses[0], lses[1]), lses[2])
        wts = [jnp.exp(l - top) for l in lses]
        o_ref[0, 0, rows, :] = _bf((wts[0] * outs[0] + wts[1] * outs[1] + wts[2] * outs[2])
                                   / (wts[0] + wts[1] + wts[2]))
        return carry

    lax.fori_loop(0, seq // w, merge, 0)


def _attn_call(q_views, k_views, v_views):
    b, pairs, s, _ = q_views[0].shape
    in_specs, args = [], []
    for pi in range(len(DILATED_PATTERNS)):
        for views in (q_views, k_views, v_views):
            in_specs.append(pl.BlockSpec((1, 1) + views[pi].shape[2:], lambda i, j: (i, j, 0, 0)))
            args.append(views[pi])
    masks = _attn_masks()
    in_specs += [pl.BlockSpec(m.shape, lambda i, j: (0, 0)) for m in masks]
    return pl.pallas_call(
        functools.partial(_attn_body, s),
        grid=(b, pairs),
        in_specs=in_specs,
        out_specs=pl.BlockSpec((1, 1, s, LANES), lambda i, j: (i, j, 0, 0)),
        out_shape=jax.ShapeDtypeStruct((b, pairs, s, LANES), BF16),
        scratch_shapes=[pltpu.VMEM((s, LANES), F32)] * 6,
        compiler_params=pltpu.CompilerParams(
            dimension_semantics=("parallel", "parallel"), vmem_limit_bytes=V7X_VMEM_LIMIT_BYTES),
        name="dilated_attn",
    )(*args, *masks)


(P_MU_R, P_MU_K, P_MU_V, P_MU_W, P_MU_A, P_MU_G, P_W0, P_A0, P_KK, P_KA, P_RK, P_LNW, P_LNB) = range(13)
P_ROWS = 16


def _rwkv_consts():
    c = RWKV_CHUNK
    assert c == HEAD_DIM
    t = np.arange(SLAB)
    tri = ((t[:, None] // c == t[None, :] // c) & (t[None, :] <= t[:, None]))
    blockdiag = (np.arange(SLAB)[:, None] // HEAD_DIM == np.arange(SLAB)[None, :] // HEAD_DIM)
    src = np.arange(SLAB_HEADS * c) % c
    tok = np.arange(c)[:, None]
    return (jnp.asarray(tri, BF16),
            jnp.asarray(blockdiag, F32),
            jnp.asarray(blockdiag, BF16),
            jnp.asarray(src[None, :] < tok, F32),
            jnp.asarray(src[None, :] <= tok, F32),
            jnp.asarray(src[None, :] == tok, F32))


def _rwkv_body(rin_ref, p_ref, w1_ref, w2_ref, a1_ref, a2_ref, g1_ref, g2_ref,
               tri_ref, bd_ref, bd16_ref, lt_ref, le_ref, eye_ref, o_ref, state_ref, prev_ref):
    c = RWKV_CHUNK
    nb, tl, w4 = rin_ref.shape
    w = o_ref.shape[-1]
    n_slabs = w // SLAB
    n_sub = tl // c
    step = pl.program_id(1)

    @pl.when(step == 0)
    def _():
        state_ref[...] = jnp.zeros_like(state_ref)
        prev_ref[...] = jnp.zeros_like(prev_ref)

    x = rin_ref[...].reshape(nb * tl, w4)
    row = lax.broadcasted_iota(jnp.int32, (nb * tl, 1), 0)
    x_prev = pltpu.roll(x, 1, 0)
    for bi in range(nb):
        x_prev = jnp.where(row == bi * tl, prev_ref[bi, 0:1, :], x_prev)
    for bi in range(nb):
        prev_ref[bi, 0:1, :] = rin_ref[bi, tl - 1:tl, :]

    def par(prow):
        return p_ref[prow:prow + 1, :]

    def sect(a, i):
        return a[:, i * w:(i + 1) * w]

    def lerp(i, mu_row):
        cur, prev = sect(x, i), sect(x_prev, i)
        return cur + (prev - cur) * par(mu_row)

    r = lerp(0, P_MU_R)
    k = lerp(1, P_MU_K)
    v = lerp(2, P_MU_V)
    cw = lerp(3, P_MU_W)
    ca = lerp(3, P_MU_A)
    cg = lerp(3, P_MU_G)

    bd = bd_ref[...]
    bd16 = bd16_ref[...]

    def head_sum(z, parts):
        return _head_sums(z, bd16, parts)

    zw = par(P_W0) + _dot(_bf(jnp.tanh(_dot(_bf(cw), w1_ref[...]))), w2_ref[...])
    log_w = -math.exp(-0.5) * jax.nn.sigmoid(zw)
    a_gate = jax.nn.sigmoid(par(P_A0) + _dot(_bf(_dot(_bf(ca), a1_ref[...])), a2_ref[...]))
    gate = _dot(_bf(jax.nn.sigmoid(_dot(_bf(cg), g1_ref[...]))), g2_ref[...])

    kk = k * par(P_KK)
    kk = kk / jnp.maximum(jnp.sqrt(head_sum(kk * kk, 2)), 1e-12)
    k = k * (1.0 + (a_gate - 1.0) * par(P_KA))
    b_vec = kk * a_gate
    bonus = head_sum(r * k * par(P_RK), 1) * v

    tri = tri_ref[...]
    cum = jnp.concatenate([_split_dot_left(tri, log_w[i:i + SLAB], 2) for i in range(0, nb * tl, SLAB)], axis=0)
    strictly_lower = lt_ref[...] > 0.0
    lower = le_ref[...] > 0.0
    eye = eye_ref[...]

    def block_diag(z):
        return jnp.concatenate([_bf(z)] * SLAB_HEADS, axis=0) * bd16

    probs = [(bi, ci, si) for bi in range(nb) for ci in range(n_sub) for si in range(n_slabs)]
    pre = {}
    for bi in range(nb):
        for ci in range(n_sub):
            rows = slice((bi * n_sub + ci) * c, (bi * n_sub + ci + 1) * c)
            lw_c, cum_c = log_w[rows], cum[rows]
            total = jnp.sum(lw_c, axis=0, keepdims=True)
            grow = jnp.exp(-cum_c)
            dec_out = jnp.exp(total - cum_c)
            a_t = -kk[rows] * jnp.exp(cum_c - lw_c)
            r_t = r[rows] * jnp.exp(cum_c)
            b_t, k_t = b_vec[rows] * grow, k[rows] * grow
            b_o, k_o = b_vec[rows] * dec_out, k[rows] * dec_out
            w_rows = jnp.broadcast_to(jnp.exp(total), (2 * c, w))
            for si in range(n_slabs):
                ln = slice(si * SLAB, (si + 1) * SLAB)
                ar = _bf(jnp.concatenate([a_t[:, ln], r_t[:, ln]], axis=0))
                g_b = _dot_nt(ar, block_diag(b_t[:, ln]))
                g_k = _dot_nt(ar, block_diag(k_t[:, ln]))
                a_k = jnp.concatenate([jnp.where(strictly_lower, g_k[:c], 0.0),
                                       jnp.where(lower, g_k[c:], 0.0)], axis=0)
                turned = jnp.transpose(jnp.concatenate([b_o[:, ln], k_o[:, ln], w_rows[:, ln]], axis=0))
                pre[bi, ci, si] = dict(
                    ar=ar, v=v[rows, ln],
                    a_ab=jnp.where(strictly_lower, g_b[:c], 0.0),
                    a_rb=_bf(jnp.where(lower, g_b[c:], 0.0)),
                    from_v=_dot(_bf(a_k), block_diag(v[rows, ln])),
                    out_t=_bf(turned[:, :2 * c]),
                    w_col=turned[:, 2 * c:2 * c + 1])

    inv = {p: eye + pre[p]["a_ab"] for p in probs}
    power = {p: _dot(_bf(pre[p]["a_ab"]), block_diag(pre[p]["a_ab"])) for p in probs}
    covered = 2
    while covered * 2 < c:
        for p in probs:
            both = _dot(_bf(jnp.concatenate([inv[p], power[p]], axis=0)), block_diag(power[p]))
            inv[p] = inv[p] + both[:c]
            power[p] = both[c:]
        covered *= 2
    for p in probs:
        inv[p] = _bf(inv[p] + _dot(_bf(inv[p]), block_diag(power[p])))

    states = {(bi, si): state_ref[bi * n_slabs + si] for bi in range(nb) for si in range(n_slabs)}
    y_parts = {}
    chains = list(states)
    for ci in range(n_sub):
        pp = {ch: pre[ch[0], ci, ch[1]] for ch in chains}
        from_state = {ch: _dot(pp[ch]["ar"], _bf(states[ch])) for ch in chains}
        rhs = {ch: block_diag(from_state[ch][:c] + pp[ch]["from_v"][:c]) for ch in chains}
        u = {ch: _dot(inv[ch[0], ci, ch[1]], rhs[ch]) for ch in chains}
        for ch in chains:
            states[ch] = states[ch] * pp[ch]["w_col"] + bd * _dot(
                pp[ch]["out_t"], _bf(jnp.concatenate([u[ch], pp[ch]["v"]], axis=0)))
        for ch in chains:
            y_parts[ch[0], ci, ch[1]] = (from_state[ch][c:] + pp[ch]["from_v"][c:]
                                         + _dot(pp[ch]["a_rb"], block_diag(u[ch])))
    for bi in range(nb):
        for si in range(n_slabs):
            state_ref[bi * n_slabs + si] = states[bi, si]
    y = jnp.concatenate(
        [jnp.concatenate([y_parts[bi, ci, si] for si in range(n_slabs)], axis=1)
         for bi in range(nb) for ci in range(n_sub)], axis=0)

    inv_n = 1.0 / HEAD_DIM
    centered = y - head_sum(y, 1) * inv_n
    var = head_sum(centered * centered, 1) * inv_n
    y = centered * lax.rsqrt(var + GN_EPS) * par(P_LNW) + par(P_LNB)
    o_ref[...] = _bf((y + bonus) * gate).reshape(nb, tl, w)


def _pad_lora(down, up):
    rank = down.shape[1]
    return (_bf(jnp.pad(down, ((0, 0), (0, LORA_PAD - rank)))),
            _bf(jnp.pad(up, ((0, LORA_PAD - rank), (0, 0)))))


def _rwkv_call(rin, chan_params, loras):
    b, s, w4 = rin.shape
    w = w4 // 4
    tl = RWKV_TOKENS
    nb = RWKV_SEQS if b % RWKV_SEQS == 0 else 1
    packed = jnp.zeros((P_ROWS, w), F32).at[:len(chan_params)].set(jnp.stack(chan_params))
    lora_args = []
    for down, up in loras:
        lora_args += list(_pad_lora(down, up))
    assert tl % SLAB == 0
    consts = _rwkv_consts()
    fixed = lambda i, j: (0, 0)
    return pl.pallas_call(
        _rwkv_body,
        grid=(b // nb, s // tl),
        in_specs=[pl.BlockSpec((nb, tl, w4), lambda i, j: (i, j, 0)), pl.BlockSpec((P_ROWS, w), fixed)]
        + [pl.BlockSpec(a.shape, fixed) for a in lora_args]
        + [pl.BlockSpec(a.shape, fixed) for a in consts],
        out_specs=pl.BlockSpec((nb, tl, w), lambda i, j: (i, j, 0)),
        out_shape=jax.ShapeDtypeStruct((b, s, w), BF16),
        scratch_shapes=[pltpu.VMEM((nb * (w // SLAB), SLAB, SLAB), F32), pltpu.VMEM((nb, 8, w4), F32)],
        compiler_params=pltpu.CompilerParams(
            dimension_semantics=("parallel", "arbitrary"), vmem_limit_bytes=V7X_VMEM_LIMIT_BYTES),
        name="rwkv7",
    )(rin, packed, *lora_args, *consts)


def kernel(x, ffn1_norm, ffn1_w_gate, ffn1_w_up, ffn1_w_down, mix_norm, w_in, q_norm, k_norm, mu_r, mu_k, mu_v, mu_w, mu_a, mu_g, w0, w1, w2, a0, a1, a2, g1, g2, k_k, k_a, r_k, ln_x_w, ln_x_b, w_out, ffn2_norm, ffn2_w_gate, ffn2_w_up, ffn2_w_down):
    b, s, d = x.shape
    depth = w_in.shape[0]
    rw_width = mu_r.shape[-1]
    att_width = (w_in.shape[-1] - 4 * rw_width) // 3
    h = x.reshape(b * s, d)
    for l in range(depth):
        h = _ffn_call(h, ffn1_norm[l], ffn1_w_gate, ffn1_w_up, ffn1_w_down, l)
        outs = _in_proj_call(h.reshape(b, s, d), mix_norm[l], _bf(w_in[l]), q_norm[l], k_norm[l], att_width)
        att = _attn_call(outs[0:3], outs[3:6], outs[6:9])
        rin = outs[9]
        chan = [mu_r[l], mu_k[l], mu_v[l], mu_w[l], mu_a[l], mu_g[l], w0[l], a0[l], k_k[l], k_a[l],
                r_k[l].reshape(rw_width), ln_x_w[l], ln_x_b[l]]
        rw = _rwkv_call(rin, chan, [(w1[l], w2[l]), (a1[l], a2[l]), (g1[l], g2[l])])
        h = _ffn_call(h, ffn2_norm[l], ffn2_w_gate, ffn2_w_up, ffn2_w_down, l,
                      mix=(att, rw.reshape(b * s, rw_width), w_out))
    return h.reshape(b, s, d)
```

```python
import functools
import math

import jax
import jax.numpy as jnp
import numpy as np
from jax import lax
from jax.experimental import pallas as pl
from jax.experimental.pallas import tpu as pltpu

F32 = jnp.float32
BF16 = jnp.bfloat16

HEAD_DIM = 64
LANES = 128
DILATED_PATTERNS = ((128, 1), (512, 4), (2048, 16))
ATT_WINDOW = 128
ATT_GROUP = 4
ATT_LOOP_GROUP = 3
FFN_RESIDUAL = 0.5
RMS_EPS = 1e-6
GN_EPS = 64e-5
NEG_INF = -1e30
LORA_PAD = 128

V7X_VMEM_LIMIT_BYTES = 60000 * 1024

FFN_TOKENS = 1024
FFN_HIDDEN_CHUNK = 512
WEIGHT_LOAD_ROWS = 256
IN_TOKENS = 512
RWKV_TOKENS = 256
RWKV_SEQS = 2
RWKV_CHUNK = 64
SLAB_HEADS = 4
SLAB = SLAB_HEADS * HEAD_DIM


def _bf(x):
    return x.astype(BF16)


def _dot(a, b):
    return jnp.dot(a, b, preferred_element_type=F32)


def _dot_nt(a, b):
    return lax.dot_general(a, b, (((1,), (1,)), ((), ())), preferred_element_type=F32)


def _dot_tn(a, b):
    return lax.dot_general(a, b, (((0,), (0,)), ((), ())), preferred_element_type=F32)


def _split_dot(x, w, parts):
    out = None
    rem = x
    for p in range(parts):
        piece = _bf(rem)
        term = _dot(piece, w)
        out = term if out is None else out + term
        if p + 1 < parts:
            rem = rem - piece.astype(F32)
    return out


def _split_dot_left(w, x, parts):
    out = None
    rem = x
    for p in range(parts):
        piece = _bf(rem)
        term = _dot(w, piece)
        out = term if out is None else out + term
        if p + 1 < parts:
            rem = rem - piece.astype(F32)
    return out


def _head_sums(z, membership, parts):
    g = membership.shape[0]
    return jnp.concatenate([_split_dot(z[:, i:i + g], membership, parts) for i in range(0, z.shape[1], g)],
                           axis=1)


def _rms_norm_rows(x, gain):
    ms = jnp.mean(x * x, axis=-1, keepdims=True)
    return x * lax.rsqrt(ms + RMS_EPS) * gain


def _load_as_bf16(src_hbm, dst_ref, stage_ref, sem_ref):
    n_rows, cols = src_hbm.shape
    rows = stage_ref.shape[1]
    assert n_rows % rows == 0
    n = n_rows // rows

    def copy(i):
        slot = i % 2
        return pltpu.make_async_copy(src_hbm.at[pl.ds(i * rows, rows), :],
                                     stage_ref.at[slot, :, pl.ds(0, cols)], sem_ref.at[slot])

    copy(0).start()
    for i in range(n):
        if i + 1 < n:
            copy(i + 1).start()
        copy(i).wait()
        dst_ref[pl.ds(i * rows, rows), :] = _bf(stage_ref[i % 2, :, pl.ds(0, cols)])


def _ffn_body(with_mix, ff_chunks, layer, *refs):
    if with_mix:
        (x_ref, att_ref, rw_ref, nw_ref, wo_hbm, wg_hbm, wu_hbm, wd_hbm, o_ref,
         wo_ref, wg_ref, wu_ref, wd_ref, stage_ref, sem_ref) = refs
    else:
        x_ref, nw_ref, wg_hbm, wu_hbm, wd_hbm, o_ref, wg_ref, wu_ref, wd_ref, stage_ref, sem_ref = refs

    @pl.when(pl.program_id(0) == 0)
    def _():
        if with_mix:
            _load_as_bf16(wo_hbm.at[layer], wo_ref, stage_ref, sem_ref)
        _load_as_bf16(wg_hbm.at[layer], wg_ref, stage_ref, sem_ref)
        _load_as_bf16(wu_hbm.at[layer], wu_ref, stage_ref, sem_ref)
        _load_as_bf16(wd_hbm.at[layer], wd_ref, stage_ref, sem_ref)

    x = x_ref[...]
    if with_mix:
        mixed = jnp.concatenate([att_ref[0, p] for p in range(att_ref.shape[1])] + [rw_ref[...]], axis=1)
        x = x + _dot(mixed, wo_ref[...])
    h = _bf(_rms_norm_rows(x, nw_ref[...]))
    acc = None
    for lo, hi in ff_chunks:
        g = _dot(h, wg_ref[:, lo:hi])
        u = _dot(h, wu_ref[:, lo:hi])
        part = _dot(_bf(g * jax.nn.sigmoid(g) * u), wd_ref[lo:hi, :])
        acc = part if acc is None else acc + part
    o_ref[...] = x + FFN_RESIDUAL * acc


def _ff_chunks(d_ff):
    bounds = list(range(0, d_ff, FFN_HIDDEN_CHUNK)) + [d_ff]
    return tuple(zip(bounds[:-1], bounds[1:]))


def _ffn_call(x, norm_w, wg, wu, wd, layer, mix=None):
    t, d = x.shape
    d_ff = wg.shape[2]
    tm = FFN_TOKENS
    row = lambda i: (i, 0)
    hbm = pl.BlockSpec(memory_space=pl.ANY)
    in_specs = [pl.BlockSpec((tm, d), row)]
    args = [x]
    scratch = []
    if mix is not None:
        att, rw, w_out = mix
        _, pairs, s, _ = att.shape
        tiles = s // tm
        in_specs += [pl.BlockSpec((1, pairs, tm, LANES), lambda i: (i // tiles, 0, i % tiles, 0)),
                     pl.BlockSpec((tm, rw.shape[1]), row)]
        args += [att, rw]
    in_specs.append(pl.BlockSpec((1, d), lambda i: (0, 0)))
    args.append(norm_w.reshape(1, d))
    if mix is not None:
        in_specs.append(hbm)
        args.append(w_out)
        scratch.append(pltpu.VMEM(w_out.shape[1:], BF16))
    in_specs += [hbm, hbm, hbm]
    args += [wg, wu, wd]
    scratch += [pltpu.VMEM((d, d_ff), BF16), pltpu.VMEM((d, d_ff), BF16), pltpu.VMEM((d_ff, d), BF16),
                pltpu.VMEM((2, WEIGHT_LOAD_ROWS, max(d, d_ff)), F32), pltpu.SemaphoreType.DMA((2,))]
    return pl.pallas_call(
        functools.partial(_ffn_body, mix is not None, _ff_chunks(d_ff), layer),
        grid=(t // tm,),
        in_specs=in_specs,
        out_specs=pl.BlockSpec((tm, d), row),
        out_shape=jax.ShapeDtypeStruct((t, d), F32),
        scratch_shapes=scratch,
        compiler_params=pltpu.CompilerParams(
            dimension_semantics=("arbitrary",), vmem_limit_bytes=V7X_VMEM_LIMIT_BYTES),
        name="ffn_mix" if mix is not None else "ffn",
    )(*args)


def _in_proj_body(att_width, layer, x_ref, nw_ref, w_hbm, qg_ref, kg_ref, hs_ref, *out_refs):
    view_refs, rin_ref, stage_refs = out_refs[:9], out_refs[9], out_refs[10:-3]
    w_ref, w_stage_ref, sem_ref = out_refs[-3:]

    @pl.when((pl.program_id(0) == 0) & (pl.program_id(1) == 0))
    def _():
        _load_as_bf16(w_hbm.at[layer], w_ref, w_stage_ref, sem_ref)

    h = _bf(_rms_norm_rows(x_ref[0], nw_ref[...]))
    aw = att_width
    proj = _dot(h, w_ref[:, :3 * aw])
    hs = hs_ref[...]
    tm = proj.shape[0]

    def head_rms(z, gain):
        ms = _head_sums(z * z, hs, 2) * (1.0 / HEAD_DIM)
        return z * lax.rsqrt(ms + RMS_EPS) * gain

    tensors = (head_rms(proj[:, 0:aw], qg_ref[...]) * (HEAD_DIM ** -0.5),
               head_rms(proj[:, aw:2 * aw], kg_ref[...]),
               proj[:, 2 * aw:3 * aw])
    rw_cols = rin_ref.shape[-1]
    rw_step = rw_cols // (len(tensors) + 1)

    def rw_part(i):
        cols = slice(i * rw_step, (i + 1) * rw_step)
        rin_ref[0, :, cols] = _dot(h, w_ref[:, 3 * aw + cols.start:3 * aw + cols.stop])

    pairs = aw // LANES
    dils = [dil for _, dil in DILATED_PATTERNS]
    assert dils[0] == 1 and len(stage_refs) == len(dils) - 1
    rw_part(len(tensors))
    for ti, val in enumerate(tensors):
        rw_part(ti)
        for p in range(pairs):
            nat = val[:, p * LANES:(p + 1) * LANES]
            stage_refs[0][p] = nat
            view_refs[3 * ti][0, p] = _bf(nat)
        for li in range(1, len(dils)):
            prev_dil, dil = dils[li - 1], dils[li]
            ratio = dil // prev_dil
            for p in range(pairs):
                for cp in range(prev_dil):
                    for c2 in range(ratio):
                        c = cp + prev_dil * c2
                        blk = stage_refs[li - 1][p * prev_dil + cp, pl.ds(c2, tm // dil, stride=ratio), :]
                        if li + 1 < len(dils):
                            stage_refs[li][p * dil + c] = blk
                        view_refs[3 * ti + li][0, p, :, c * LANES:(c + 1) * LANES] = _bf(blk)


def _head_sum_matrix(width):
    head = np.arange(width) // HEAD_DIM
    return jnp.asarray(head[:, None] == head[None, :], dtype=BF16)


def _in_proj_call(x, norm_w, w_in, layer, q_gain, k_gain, att_width):
    b, s, d = x.shape
    cols = w_in.shape[2]
    rw_cols = cols - 3 * att_width
    tm = IN_TOKENS
    heads = att_width // HEAD_DIM
    pairs = att_width // LANES
    fixed = lambda i, j: (0, 0)
    view_shapes, view_specs = [], []
    for _, dil in DILATED_PATTERNS:
        view_shapes.append(jax.ShapeDtypeStruct((b, pairs, s // dil, dil * LANES), BF16))
        view_specs.append(pl.BlockSpec((1, pairs, tm // dil, dil * LANES), lambda i, j: (i, 0, j, 0)))
    return pl.pallas_call(
        functools.partial(_in_proj_body, att_width, layer),
        grid=(b, s // tm),
        in_specs=[pl.BlockSpec((1, tm, d), lambda i, j: (i, j, 0)), pl.BlockSpec((1, d), fixed),
                  pl.BlockSpec(memory_space=pl.ANY),
                  pl.BlockSpec((1, att_width), fixed), pl.BlockSpec((1, att_width), fixed),
                  pl.BlockSpec((SLAB, SLAB), fixed)],
        out_specs=view_specs * 3 + [pl.BlockSpec((1, tm, rw_cols), lambda i, j: (i, j, 0))],
        out_shape=view_shapes * 3 + [jax.ShapeDtypeStruct((b, s, rw_cols), F32)],
        scratch_shapes=[pltpu.VMEM((pairs * dil, tm // dil, LANES), F32) for _, dil in DILATED_PATTERNS[:-1]]
        + [pltpu.VMEM((d, cols), BF16), pltpu.VMEM((2, WEIGHT_LOAD_ROWS, cols), F32),
           pltpu.SemaphoreType.DMA((2,))],
        compiler_params=pltpu.CompilerParams(
            dimension_semantics=("arbitrary", "arbitrary"), vmem_limit_bytes=V7X_VMEM_LIMIT_BYTES),
        name="in_proj",
    )(x, norm_w.reshape(1, d), w_in, jnp.tile(q_gain, heads).reshape(1, att_width),
      jnp.tile(k_gain, heads).reshape(1, att_width), _head_sum_matrix(SLAB))


def _attn_masks():
    w = ATT_WINDOW
    r = np.arange(w)[:, None]
    causal = np.arange(w)[None, :] <= r
    u = np.arange(2 * w)[None, :]
    band = (u >= r) & (u <= r + w)
    two_heads = lambda m: jnp.asarray(np.concatenate([m, m], axis=0), F32)
    return two_heads(causal), two_heads(band)


def _attn_body(seq, qn_ref, kn_ref, vn_ref, q4_ref, k4_ref, v4_ref, q16_ref, k16_ref, v16_ref,
               causal_ref, band_ref, o_ref, *part_refs):
    w = ATT_WINDOW
    lower_head = lax.broadcasted_iota(jnp.int32, (1, LANES), 1) < HEAD_DIM
    causal = causal_ref[...] > 0.0
    band = band_ref[...] > 0.0

    def pick(two):
        return jnp.where(lower_head, two[:w], two[w:])

    def blocks(tasks):
        qq = []
        for q, _, _, _ in tasks:
            zero = jnp.zeros_like(q)
            qq.append(jnp.concatenate([jnp.where(lower_head, q, zero), jnp.where(lower_head, zero, q)], axis=0))
        s = [jnp.where(t[3], _dot_nt(a, t[1]), NEG_INF) for a, t in zip(qq, tasks)]
        m = [jnp.max(x, axis=1, keepdims=True) for x in s]
        p = [_bf(jnp.exp(x - mx)) for x, mx in zip(s, m)]
        ol = [_dot(x, jnp.concatenate([t[2], jnp.ones_like(t[2])], axis=1)) for x, t in zip(p, tasks)]
        results = []
        for oli, mi in zip(ol, m):
            l_full = pick(oli[:, LANES:])
            m_full = pick(jnp.broadcast_to(mi, (2 * w, LANES)))
            results.append((pick(oli[:, :LANES]) / l_full, m_full + jnp.log(l_full)))
        return results

    def run_pattern(q_ref, k_ref, v_ref, dil, out_ref, lse_ref):
        n_blocks = seq // dil // w

        def offset(nb, size):
            return nb * size if isinstance(nb, int) else pl.multiple_of(nb * size, size)

        def task(c, nb):
            lanes = slice(c * LANES, (c + 1) * LANES)
            if isinstance(nb, int) and nb == 0:
                rows = slice(0, w)
                return q_ref[0, 0, rows, lanes], k_ref[0, 0, rows, lanes], v_ref[0, 0, rows, lanes], causal
            q_rows = pl.ds(offset(nb, w), w)
            k_rows = pl.ds(offset(nb - 1, w), 2 * w)
            return q_ref[0, 0, q_rows, lanes], k_ref[0, 0, k_rows, lanes], v_ref[0, 0, k_rows, lanes], band

        def store(c, nb, result):
            if dil > 1:
                rows = pl.ds(nb * (w * dil) + c, w, stride=dil)
                out_ref[rows, :], lse_ref[rows, :] = result
            else:
                merge(pl.ds(offset(nb, w), w), *result)

        def run(group):
            for (c, nb), result in zip(group, blocks([task(c, nb) for c, nb in group])):
                store(c, nb, result)

        looped = n_blocks > ATT_GROUP
        static = [(c, nb) for c in range(dil) for nb in range(1 if looped else n_blocks)]
        for g in range(0, len(static), ATT_GROUP):
            run(static[g:g + ATT_GROUP])
        if looped:
            assert (n_blocks - 1) % ATT_LOOP_GROUP == 0
            for c in range(dil):
                def trip(i, carry):
                    run([(c, 1 + i * ATT_LOOP_GROUP + t) for t in range(ATT_LOOP_GROUP)])
                    return carry

                lax.fori_loop(0, (n_blocks - 1) // ATT_LOOP_GROUP, trip, 0)

    def merge(rows, out_nat, lse_nat):
        outs = [out_nat] + [part_refs[2 * pi][rows, :] for pi in range(n_strided)]
        lses = [lse_nat] + [part_refs[2 * pi + 1][rows, :] for pi in range(n_strided)]
        top = functools.reduce(jnp.maximum, lses)
        wts = [jnp.exp(l - top) for l in lses]
        o_ref[0, 0, rows, :] = _bf(sum(wt * o for wt, o in zip(wts, outs)) / sum(wts))

    views = ((qn_ref, kn_ref, vn_ref), (q4_ref, k4_ref, v4_ref), (q16_ref, k16_ref, v16_ref))
    strided = [(dil, v) for (window, dil), v in zip(DILATED_PATTERNS, views) if dil > 1]
    n_strided = len(strided)
    assert all(window // dil == w for window, dil in DILATED_PATTERNS) and n_strided == len(views) - 1
    for pi, (dil, (q_ref, k_ref, v_ref)) in enumerate(strided):
        run_pattern(q_ref, k_ref, v_ref, dil, part_refs[2 * pi], part_refs[2 * pi + 1])
    for (window, dil), (q_ref, k_ref, v_ref) in zip(DILATED_PATTERNS, views):
        if dil == 1:
            run_pattern(q_ref, k_ref, v_ref, dil, None, None)


def _attn_call(q_views, k_views, v_views):
    b, pairs, s, _ = q_views[0].shape
    in_specs, args = [], []
    for pi in range(len(DILATED_PATTERNS)):
        for views in (q_views, k_views, v_views):
            in_specs.append(pl.BlockSpec((1, 1) + views[pi].shape[2:], lambda i, j: (i, j, 0, 0)))
            args.append(views[pi])
    masks = _attn_masks()
    in_specs += [pl.BlockSpec(m.shape, lambda i, j: (0, 0)) for m in masks]
    return pl.pallas_call(
        functools.partial(_attn_body, s),
        grid=(b, pairs),
        in_specs=in_specs,
        out_specs=pl.BlockSpec((1, 1, s, LANES), lambda i, j: (i, j, 0, 0)),
        out_shape=jax.ShapeDtypeStruct((b, pairs, s, LANES), BF16),
        scratch_shapes=[pltpu.VMEM((s, LANES), F32)] * (2 * (len(DILATED_PATTERNS) - 1)),
        compiler_params=pltpu.CompilerParams(
            dimension_semantics=("parallel", "parallel"), vmem_limit_bytes=V7X_VMEM_LIMIT_BYTES),
        name="dilated_attn",
    )(*args, *masks)


(P_MU_R, P_MU_K, P_MU_V, P_MU_W, P_MU_A, P_MU_G, P_W0, P_A0, P_KK, P_KA, P_RK, P_LNW, P_LNB) = range(13)
P_ROWS = 16


def _rwkv_consts():
    c = RWKV_CHUNK
    assert c == HEAD_DIM
    t = np.arange(SLAB)
    tri = ((t[:, None] // c == t[None, :] // c) & (t[None, :] <= t[:, None]))
    blockdiag = (np.arange(SLAB)[:, None] // HEAD_DIM == np.arange(SLAB)[None, :] // HEAD_DIM)
    src = np.arange(SLAB_HEADS * c) % c
    tok = np.arange(c)[:, None]
    return (jnp.asarray(tri, BF16),
            jnp.asarray(blockdiag, F32),
            jnp.asarray(blockdiag, BF16),
            jnp.asarray(src[None, :] < tok, F32),
            jnp.asarray(src[None, :] <= tok, F32),
            jnp.asarray(src[None, :] == tok, F32))


def _rwkv_body(rin_ref, p_ref, w1_ref, w2_ref, a1_ref, a2_ref, g1_ref, g2_ref,
               tri_ref, bd_ref, bd16_ref, lt_ref, le_ref, eye_ref, o_ref, state_ref, prev_ref):
    c = RWKV_CHUNK
    nb, tl, w4 = rin_ref.shape
    w = o_ref.shape[-1]
    n_slabs = w // SLAB
    n_sub = tl // c
    step = pl.program_id(1)

    @pl.when(step == 0)
    def _():
        state_ref[...] = jnp.zeros_like(state_ref)
        prev_ref[...] = jnp.zeros_like(prev_ref)

    x = rin_ref[...].reshape(nb * tl, w4)
    row = lax.broadcasted_iota(jnp.int32, (nb * tl, 1), 0)
    x_prev = pltpu.roll(x, 1, 0)
    for bi in range(nb):
        x_prev = jnp.where(row == bi * tl, prev_ref[bi, 0:1, :], x_prev)
    for bi in range(nb):
        prev_ref[bi, 0:1, :] = rin_ref[bi, tl - 1:tl, :]

    def par(prow):
        return p_ref[prow:prow + 1, :]

    def sect(a, i):
        return a[:, i * w:(i + 1) * w]

    def lerp(i, mu_row):
        cur, prev = sect(x, i), sect(x_prev, i)
        return cur + (prev - cur) * par(mu_row)

    r = lerp(0, P_MU_R)
    k = lerp(1, P_MU_K)
    v = lerp(2, P_MU_V)
    cw = lerp(3, P_MU_W)
    ca = lerp(3, P_MU_A)
    cg = lerp(3, P_MU_G)

    bd = bd_ref[...]
    bd16 = bd16_ref[...]

    def head_sum(z, parts):
        return _head_sums(z, bd16, parts)

    zw = par(P_W0) + _dot(_bf(jnp.tanh(_dot(_bf(cw), w1_ref[...]))), w2_ref[...])
    log_w = -math.exp(-0.5) * jax.nn.sigmoid(zw)
    a_gate = jax.nn.sigmoid(par(P_A0) + _dot(_bf(_dot(_bf(ca), a1_ref[...])), a2_ref[...]))
    gate = _dot(_bf(jax.nn.sigmoid(_dot(_bf(cg), g1_ref[...]))), g2_ref[...])

    kk = k * par(P_KK)
    kk = kk / jnp.maximum(jnp.sqrt(head_sum(kk * kk, 2)), 1e-12)
    k = k * (1.0 + (a_gate - 1.0) * par(P_KA))
    b_vec = kk * a_gate
    bonus = head_sum(r * k * par(P_RK), 1) * v

    tri = tri_ref[...]
    cum = jnp.concatenate([_split_dot_left(tri, log_w[i:i + SLAB], 2) for i in range(0, nb * tl, SLAB)], axis=0)
    strictly_lower = lt_ref[...] > 0.0
    lower = le_ref[...] > 0.0
    eye = eye_ref[...]

    def block_diag(z):
        return jnp.concatenate([_bf(z)] * SLAB_HEADS, axis=0) * bd16

    probs = [(bi, ci, si) for bi in range(nb) for ci in range(n_sub) for si in range(n_slabs)]
    pre = {}
    for bi in range(nb):
        for ci in range(n_sub):
            rows = slice((bi * n_sub + ci) * c, (bi * n_sub + ci + 1) * c)
            lw_c, cum_c = log_w[rows], cum[rows]
            total = jnp.sum(lw_c, axis=0, keepdims=True)
            grow = jnp.exp(-cum_c)
            dec_out = jnp.exp(total - cum_c)
            a_t = -kk[rows] * jnp.exp(cum_c - lw_c)
            r_t = r[rows] * jnp.exp(cum_c)
            b_t, k_t = b_vec[rows] * grow, k[rows] * grow
            b_o, k_o = b_vec[rows] * dec_out, k[rows] * dec_out
            w_rows = jnp.broadcast_to(jnp.exp(total), (2 * c, w))
            for si in range(n_slabs):
                ln = slice(si * SLAB, (si + 1) * SLAB)
                ar = _bf(jnp.concatenate([a_t[:, ln], r_t[:, ln]], axis=0))
                g_b = _dot_nt(ar, block_diag(b_t[:, ln]))
                g_k = _dot_nt(ar, block_diag(k_t[:, ln]))
                a_k = jnp.concatenate([jnp.where(strictly_lower, g_k[:c], 0.0),
                                       jnp.where(lower, g_k[c:], 0.0)], axis=0)
                turned = jnp.transpose(jnp.concatenate([b_o[:, ln], k_o[:, ln], w_rows[:, ln]], axis=0))
                pre[bi, ci, si] = dict(
                    ar=ar, v=v[rows, ln],
                    a_ab=jnp.where(strictly_lower, g_b[:c], 0.0),
                    a_rb=_bf(jnp.where(lower, g_b[c:], 0.0)),
                    from_v=_dot(_bf(a_k), block_diag(v[rows, ln])),
                    out_t=_bf(turned[:, :2 * c]),
                    w_col=turned[:, 2 * c:2 * c + 1])

    inv = {p: eye + pre[p]["a_ab"] for p in probs}
    power = {p: _dot(_bf(pre[p]["a_ab"]), block_diag(pre[p]["a_ab"])) for p in probs}
    covered = 2
    while covered * 2 < c:
        for p in probs:
            both = _dot(_bf(jnp.concatenate([inv[p], power[p]], axis=0)), block_diag(power[p]))
            inv[p] = inv[p] + both[:c]
            power[p] = both[c:]
        covered *= 2
    for p in probs:
        inv[p] = _bf(inv[p] + _dot(_bf(inv[p]), block_diag(power[p])))

    states = {(bi, si): state_ref[bi * n_slabs + si] for bi in range(nb) for si in range(n_slabs)}
    y_parts = {}
    chains = list(states)
    for ci in range(n_sub):
        pp = {ch: pre[ch[0], ci, ch[1]] for ch in chains}
        from_state = {ch: _dot(pp[ch]["ar"], _bf(states[ch])) for ch in chains}
        rhs = {ch: block_diag(from_state[ch][:c] + pp[ch]["from_v"][:c]) for ch in chains}
        u = {ch: _dot(inv[ch[0], ci, ch[1]], rhs[ch]) for ch in chains}
        for ch in chains:
            states[ch] = states[ch] * pp[ch]["w_col"] + bd * _dot(
                pp[ch]["out_t"], _bf(jnp.concatenate([u[ch], pp[ch]["v"]], axis=0)))
        for ch in chains:
            y_parts[ch[0], ci, ch[1]] = (from_state[ch][c:] + pp[ch]["from_v"][c:]
                                         + _dot(pp[ch]["a_rb"], block_diag(u[ch])))
    for bi in range(nb):
        for si in range(n_slabs):
            state_ref[bi * n_slabs + si] = states[bi, si]
    y = jnp.concatenate(
        [jnp.concatenate([y_parts[bi, ci, si] for si in range(n_slabs)], axis=1)
         for bi in range(nb) for ci in range(n_sub)], axis=0)

    inv_n = 1.0 / HEAD_DIM
    centered = y - head_sum(y, 1) * inv_n
    var = head_sum(centered * centered, 1) * inv_n
    y = centered * lax.rsqrt(var + GN_EPS) * par(P_LNW) + par(P_LNB)
    o_ref[...] = _bf((y + bonus) * gate).reshape(nb, tl, w)


def _pad_lora(down, up):
    rank = down.shape[1]
    return (_bf(jnp.pad(down, ((0, 0), (0, LORA_PAD - rank)))),
            _bf(jnp.pad(up, ((0, LORA_PAD - rank), (0, 0)))))


def _rwkv_call(rin, chan_params, loras):
    b, s, w4 = rin.shape
    w = w4 // 4
    tl = RWKV_TOKENS
    nb = RWKV_SEQS if b % RWKV_SEQS == 0 else 1
    packed = jnp.zeros((P_ROWS, w), F32).at[:len(chan_params)].set(jnp.stack(chan_params))
    lora_args = []
    for down, up in loras:
        lora_args += list(_pad_lora(down, up))
    assert tl % SLAB == 0
    consts = _rwkv_consts()
    fixed = lambda i, j: (0, 0)
    return pl.pallas_call(
        _rwkv_body,
        grid=(b // nb, s // tl),
        in_specs=[pl.BlockSpec((nb, tl, w4), lambda i, j: (i, j, 0)), pl.BlockSpec((P_ROWS, w), fixed)]
        + [pl.BlockSpec(a.shape, fixed) for a in lora_args]
        + [pl.BlockSpec(a.shape, fixed) for a in consts],
        out_specs=pl.BlockSpec((nb, tl, w), lambda i, j: (i, j, 0)),
        out_shape=jax.ShapeDtypeStruct((b, s, w), BF16),
        scratch_shapes=[pltpu.VMEM((nb * (w // SLAB), SLAB, SLAB), F32), pltpu.VMEM((nb, 8, w4), F32)],
        compiler_params=pltpu.CompilerParams(
            dimension_semantics=("parallel", "arbitrary"), vmem_limit_bytes=V7X_VMEM_LIMIT_BYTES),
        name="rwkv7",
    )(rin, packed, *lora_args, *consts)


def kernel(x, ffn1_norm, ffn1_w_gate, ffn1_w_up, ffn1_w_down, mix_norm, w_in, q_norm, k_norm, mu_r, mu_k, mu_v, mu_w, mu_a, mu_g, w0, w1, w2, a0, a1, a2, g1, g2, k_k, k_a, r_k, ln_x_w, ln_x_b, w_out, ffn2_norm, ffn2_w_gate, ffn2_w_up, ffn2_w_down):
    b, s, d = x.shape
    depth = w_in.shape[0]
    rw_width = mu_r.shape[-1]
    att_width = (w_in.shape[-1] - 4 * rw_width) // 3
    h = x.reshape(b * s, d)
    for l in range(depth):
        h = _ffn_call(h, ffn1_norm[l], ffn1_w_gate, ffn1_w_up, ffn1_w_down, l)
        outs = _in_proj_call(h.reshape(b, s, d), mix_norm[l], w_in, l, q_norm[l], k_norm[l], att_width)
        att = _attn_call(outs[0:3], outs[3:6], outs[6:9])
        rin = outs[9]
        chan = [mu_r[l], mu_k[l], mu_v[l], mu_w[l], mu_a[l], mu_g[l], w0[l], a0[l], k_k[l], k_a[l],
                r_k[l].reshape(rw_width), ln_x_w[l], ln_x_b[l]]
        rw = _rwkv_call(rin, chan, [(w1[l], w2[l]), (a1[l], a2[l]), (g1[l], g2[l])])
        h = _ffn_call(h, ffn2_norm[l], ffn2_w_gate, ffn2_w_up, ffn2_w_down, l,
                      mix=(att, rw.reshape(b * s, rw_width), w_out))
    return h.reshape(b, s, d)
```

```python
import functools
import math

import jax
import jax.numpy as jnp
import numpy as np
from jax import lax
from jax.experimental import pallas as pl
from jax.experimental.pallas import tpu as pltpu

F32 = jnp.float32
BF16 = jnp.bfloat16

HEAD_DIM = 64
LANES = 128
DILATED_PATTERNS = ((128, 1), (512, 4), (2048, 16))
ATT_WINDOW = 128
ATT_GROUP = 4
ATT_LOOP_GROUP = 5
FFN_RESIDUAL = 0.5
RMS_EPS = 1e-6
GN_EPS = 64e-5
NEG_INF = -1e30
LORA_PAD = 128

V7X_VMEM_LIMIT_BYTES = 60000 * 1024

FFN_TOKENS = 1024
FFN_HIDDEN_CHUNK = 256
WEIGHT_LOAD_ROWS = 256
IN_TOKENS = 512
RWKV_TOKENS = 256
RWKV_SEQS = 2
RWKV_CHUNK = 64
SLAB_HEADS = 4
SLAB = SLAB_HEADS * HEAD_DIM


def _bf(x):
    return x.astype(BF16)


def _dot(a, b):
    return jnp.dot(a, b, preferred_element_type=F32)


def _dot_nt(a, b):
    return lax.dot_general(a, b, (((1,), (1,)), ((), ())), preferred_element_type=F32)


def _dot_tn(a, b):
    return lax.dot_general(a, b, (((0,), (0,)), ((), ())), preferred_element_type=F32)


def _split_dot_left(w, x, parts):
    out = None
    rem = x
    for p in range(parts):
        piece = _bf(rem)
        term = _dot(w, piece)
        out = term if out is None else out + term
        if p + 1 < parts:
            rem = rem - piece.astype(F32)
    return out


def _head_sums(z, membership):
    g = membership.shape[0]
    return jnp.concatenate([_dot(_bf(z[:, i:i + g]), membership) for i in range(0, z.shape[1], g)], axis=1)


def _rms_norm_rows(x, gain):
    ms = jnp.mean(x * x, axis=-1, keepdims=True)
    return x * lax.rsqrt(ms + RMS_EPS) * gain


def _load_as_bf16(src_hbm, dst_ref, stage_ref, sem_ref):
    n_rows, cols = src_hbm.shape
    rows = stage_ref.shape[1]
    assert n_rows % rows == 0
    n = n_rows // rows

    def copy(i):
        slot = i % 2
        return pltpu.make_async_copy(src_hbm.at[pl.ds(i * rows, rows), :],
                                     stage_ref.at[slot, :, pl.ds(0, cols)], sem_ref.at[slot])

    copy(0).start()
    for i in range(n):
        if i + 1 < n:
            copy(i + 1).start()
        copy(i).wait()
        dst_ref[pl.ds(i * rows, rows), :] = _bf(stage_ref[i % 2, :, pl.ds(0, cols)])


def _ffn_body(with_mix, ff_chunks, layer, *refs):
    if with_mix:
        (x_ref, att_ref, rw_ref, nw_ref, wo_hbm, wg_hbm, wu_hbm, wd_hbm, o_ref,
         wo_ref, wg_ref, wu_ref, wd_ref, stage_ref, sem_ref) = refs
    else:
        x_ref, nw_ref, wg_hbm, wu_hbm, wd_hbm, o_ref, wg_ref, wu_ref, wd_ref, stage_ref, sem_ref = refs

    @pl.when(pl.program_id(0) == 0)
    def _():
        if with_mix:
            _load_as_bf16(wo_hbm.at[layer], wo_ref, stage_ref, sem_ref)
        _load_as_bf16(wg_hbm.at[layer], wg_ref, stage_ref, sem_ref)
        _load_as_bf16(wu_hbm.at[layer], wu_ref, stage_ref, sem_ref)
        _load_as_bf16(wd_hbm.at[layer], wd_ref, stage_ref, sem_ref)

    x = x_ref[...]
    if with_mix:
        mixed = jnp.concatenate([att_ref[0, p] for p in range(att_ref.shape[1])] + [rw_ref[...]], axis=1)
        x = x + _dot(mixed, wo_ref[...])
    h = _bf(_rms_norm_rows(x, nw_ref[...]))
    acc = None
    for lo, hi in ff_chunks:
        g = _dot(h, wg_ref[:, lo:hi])
        u = _dot(h, wu_ref[:, lo:hi])
        part = _dot(_bf(g * jax.nn.sigmoid(g) * u), wd_ref[lo:hi, :])
        acc = part if acc is None else acc + part
    o_ref[...] = x + FFN_RESIDUAL * acc


def _ff_chunks(d_ff):
    bounds = list(range(0, d_ff, FFN_HIDDEN_CHUNK)) + [d_ff]
    return tuple(zip(bounds[:-1], bounds[1:]))


def _ffn_call(x, norm_w, wg, wu, wd, layer, mix=None):
    t, d = x.shape
    d_ff = wg.shape[2]
    tm = FFN_TOKENS
    row = lambda i: (i, 0)
    hbm = pl.BlockSpec(memory_space=pl.ANY)
    in_specs = [pl.BlockSpec((tm, d), row)]
    args = [x]
    scratch = []
    if mix is not None:
        att, rw, w_out = mix
        _, pairs, s, _ = att.shape
        tiles = s // tm
        in_specs += [pl.BlockSpec((1, pairs, tm, LANES), lambda i: (i // tiles, 0, i % tiles, 0)),
                     pl.BlockSpec((tm, rw.shape[1]), row)]
        args += [att, rw]
    in_specs.append(pl.BlockSpec((1, d), lambda i: (0, 0)))
    args.append(norm_w.reshape(1, d))
    if mix is not None:
        in_specs.append(hbm)
        args.append(w_out)
        scratch.append(pltpu.VMEM(w_out.shape[1:], BF16))
    in_specs += [hbm, hbm, hbm]
    args += [wg, wu, wd]
    scratch += [pltpu.VMEM((d, d_ff), BF16), pltpu.VMEM((d, d_ff), BF16), pltpu.VMEM((d_ff, d), BF16),
                pltpu.VMEM((2, WEIGHT_LOAD_ROWS, max(d, d_ff)), F32), pltpu.SemaphoreType.DMA((2,))]
    return pl.pallas_call(
        functools.partial(_ffn_body, mix is not None, _ff_chunks(d_ff), layer),
        grid=(t // tm,),
        in_specs=in_specs,
        out_specs=pl.BlockSpec((tm, d), row),
        out_shape=jax.ShapeDtypeStruct((t, d), F32),
        scratch_shapes=scratch,
        compiler_params=pltpu.CompilerParams(
            dimension_semantics=("arbitrary",), vmem_limit_bytes=V7X_VMEM_LIMIT_BYTES),
        name="ffn_mix" if mix is not None else "ffn",
    )(*args)


def _in_proj_body(att_width, layer, x_ref, nw_ref, w_hbm, qg_ref, kg_ref, hs_ref, *out_refs):
    view_refs, rin_ref, stage_refs = out_refs[:9], out_refs[9], out_refs[10:-3]
    w_ref, w_stage_ref, sem_ref = out_refs[-3:]

    @pl.when((pl.program_id(0) == 0) & (pl.program_id(1) == 0))
    def _():
        _load_as_bf16(w_hbm.at[layer], w_ref, w_stage_ref, sem_ref)

    h = _bf(_rms_norm_rows(x_ref[0], nw_ref[...]))
    aw = att_width
    proj = _dot(h, w_ref[:, :3 * aw])
    hs = hs_ref[...]
    tm = proj.shape[0]

    def head_rms(z, gain):
        ms = _head_sums(z * z, hs) * (1.0 / HEAD_DIM)
        return z * lax.rsqrt(ms + RMS_EPS) * gain

    tensors = (head_rms(proj[:, 0:aw], qg_ref[...]) * (HEAD_DIM ** -0.5),
               head_rms(proj[:, aw:2 * aw], kg_ref[...]),
               proj[:, 2 * aw:3 * aw])
    rw_cols = rin_ref.shape[-1]
    rw_step = rw_cols // (len(tensors) + 1)

    def rw_part(i):
        cols = slice(i * rw_step, (i + 1) * rw_step)
        rin_ref[0, :, cols] = _dot(h, w_ref[:, 3 * aw + cols.start:3 * aw + cols.stop])

    pairs = aw // LANES
    dils = [dil for _, dil in DILATED_PATTERNS]
    assert dils[0] == 1 and len(stage_refs) == len(dils) - 1
    rw_part(len(tensors))
    for ti, val in enumerate(tensors):
        rw_part(ti)
        for p in range(pairs):
            nat = val[:, p * LANES:(p + 1) * LANES]
            stage_refs[0][p] = nat
            view_refs[3 * ti][0, p] = _bf(nat)
        for li in range(1, len(dils)):
            prev_dil, dil = dils[li - 1], dils[li]
            ratio = dil // prev_dil
            for p in range(pairs):
                for cp in range(prev_dil):
                    for c2 in range(ratio):
                        c = cp + prev_dil * c2
                        blk = stage_refs[li - 1][p * prev_dil + cp, pl.ds(c2, tm // dil, stride=ratio), :]
                        if li + 1 < len(dils):
                            stage_refs[li][p * dil + c] = blk
                        view_refs[3 * ti + li][0, p, :, c * LANES:(c + 1) * LANES] = _bf(blk)


def _head_sum_matrix(width):
    head = np.arange(width) // HEAD_DIM
    return jnp.asarray(head[:, None] == head[None, :], dtype=BF16)


def _in_proj_call(x, norm_w, w_in, layer, q_gain, k_gain, att_width):
    b, s, d = x.shape
    cols = w_in.shape[2]
    rw_cols = cols - 3 * att_width
    tm = IN_TOKENS
    heads = att_width // HEAD_DIM
    pairs = att_width // LANES
    fixed = lambda i, j: (0, 0)
    view_shapes, view_specs = [], []
    for _, dil in DILATED_PATTERNS:
        view_shapes.append(jax.ShapeDtypeStruct((b, pairs, s // dil, dil * LANES), BF16))
        view_specs.append(pl.BlockSpec((1, pairs, tm // dil, dil * LANES), lambda i, j: (i, 0, j, 0)))
    return pl.pallas_call(
        functools.partial(_in_proj_body, att_width, layer),
        grid=(b, s // tm),
        in_specs=[pl.BlockSpec((1, tm, d), lambda i, j: (i, j, 0)), pl.BlockSpec((1, d), fixed),
                  pl.BlockSpec(memory_space=pl.ANY),
                  pl.BlockSpec((1, att_width), fixed), pl.BlockSpec((1, att_width), fixed),
                  pl.BlockSpec((SLAB, SLAB), fixed)],
        out_specs=view_specs * 3 + [pl.BlockSpec((1, tm, rw_cols), lambda i, j: (i, j, 0))],
        out_shape=view_shapes * 3 + [jax.ShapeDtypeStruct((b, s, rw_cols), F32)],
        scratch_shapes=[pltpu.VMEM((pairs * dil, tm // dil, LANES), F32) for _, dil in DILATED_PATTERNS[:-1]]
        + [pltpu.VMEM((d, cols), BF16), pltpu.VMEM((2, WEIGHT_LOAD_ROWS, cols), F32),
           pltpu.SemaphoreType.DMA((2,))],
        compiler_params=pltpu.CompilerParams(
            dimension_semantics=("arbitrary", "arbitrary"), vmem_limit_bytes=V7X_VMEM_LIMIT_BYTES),
        name="in_proj",
    )(x, norm_w.reshape(1, d), w_in, jnp.tile(q_gain, heads).reshape(1, att_width),
      jnp.tile(k_gain, heads).reshape(1, att_width), _head_sum_matrix(SLAB))


def _attn_masks():
    w = ATT_WINDOW
    r = np.arange(w)[:, None]
    causal = np.arange(w)[None, :] <= r
    u = np.arange(2 * w)[None, :]
    band = (u >= r) & (u <= r + w)
    two_heads = lambda m: jnp.asarray(np.concatenate([m, m], axis=0), F32)
    return two_heads(causal), two_heads(band)


def _attn_body(seq, qn_ref, kn_ref, vn_ref, q4_ref, k4_ref, v4_ref, q16_ref, k16_ref, v16_ref,
               causal_ref, band_ref, o_ref, *part_refs):
    w = ATT_WINDOW
    lower_head = lax.broadcasted_iota(jnp.int32, (1, LANES), 1) < HEAD_DIM
    causal = causal_ref[...] > 0.0
    band = band_ref[...] > 0.0

    def pick(two):
        return jnp.where(lower_head, two[:w], two[w:])

    def blocks(tasks):
        qq = []
        for q, _, _, _ in tasks:
            zero = jnp.zeros_like(q)
            qq.append(jnp.concatenate([jnp.where(lower_head, q, zero), jnp.where(lower_head, zero, q)], axis=0))
        s = [jnp.where(t[3], _dot_nt(a, t[1]), NEG_INF) for a, t in zip(qq, tasks)]
        m = [jnp.max(x, axis=1, keepdims=True) for x in s]
        p = [_bf(jnp.exp(x - mx)) for x, mx in zip(s, m)]
        ol = [_dot(x, jnp.concatenate([t[2], jnp.ones_like(t[2])], axis=1)) for x, t in zip(p, tasks)]
        results = []
        for oli, mi in zip(ol, m):
            l_full = pick(oli[:, LANES:])
            m_full = pick(jnp.broadcast_to(mi, (2 * w, LANES)))
            results.append((pick(oli[:, :LANES]) / l_full, m_full + jnp.log(l_full)))
        return results

    def run_pattern(q_ref, k_ref, v_ref, dil, out_ref, lse_ref):
        n_blocks = seq // dil // w

        def offset(nb, size):
            return nb * size if isinstance(nb, int) else pl.multiple_of(nb * size, size)

        def task(c, nb):
            lanes = slice(c * LANES, (c + 1) * LANES)
            if isinstance(nb, int) and nb == 0:
                rows = slice(0, w)
                return q_ref[0, 0, rows, lanes], k_ref[0, 0, rows, lanes], v_ref[0, 0, rows, lanes], causal
            q_rows = pl.ds(offset(nb, w), w)
            k_rows = pl.ds(offset(nb - 1, w), 2 * w)
            return q_ref[0, 0, q_rows, lanes], k_ref[0, 0, k_rows, lanes], v_ref[0, 0, k_rows, lanes], band

        def store(c, nb, result):
            if dil > 1:
                rows = pl.ds(nb * (w * dil) + c, w, stride=dil)
                out_ref[rows, :], lse_ref[rows, :] = result
            else:
                merge(pl.ds(offset(nb, w), w), *result)

        def run(group):
            for (c, nb), result in zip(group, blocks([task(c, nb) for c, nb in group])):
                store(c, nb, result)

        looped = n_blocks > ATT_GROUP
        static = [(c, nb) for c in range(dil) for nb in range(1 if looped else n_blocks)]
        group = ATT_GROUP * (2 if n_blocks == 1 else 1)
        for g in range(0, len(static), group):
            run(static[g:g + group])
        if looped:
            assert (n_blocks - 1) % ATT_LOOP_GROUP == 0
            for c in range(dil):
                def trip(i, carry):
                    run([(c, 1 + i * ATT_LOOP_GROUP + t) for t in range(ATT_LOOP_GROUP)])
                    return carry

                lax.fori_loop(0, (n_blocks - 1) // ATT_LOOP_GROUP, trip, 0)

    def merge(rows, out_nat, lse_nat):
        outs = [out_nat] + [part_refs[2 * pi][rows, :] for pi in range(n_strided)]
        lses = [lse_nat] + [part_refs[2 * pi + 1][rows, :] for pi in range(n_strided)]
        top = functools.reduce(jnp.maximum, lses)
        wts = [jnp.exp(l - top) for l in lses]
        o_ref[0, 0, rows, :] = _bf(sum(wt * o for wt, o in zip(wts, outs)) / sum(wts))

    views = ((qn_ref, kn_ref, vn_ref), (q4_ref, k4_ref, v4_ref), (q16_ref, k16_ref, v16_ref))
    strided = [(dil, v) for (window, dil), v in zip(DILATED_PATTERNS, views) if dil > 1]
    n_strided = len(strided)
    assert all(window // dil == w for window, dil in DILATED_PATTERNS) and n_strided == len(views) - 1
    for pi, (dil, (q_ref, k_ref, v_ref)) in enumerate(strided):
        run_pattern(q_ref, k_ref, v_ref, dil, part_refs[2 * pi], part_refs[2 * pi + 1])
    for (window, dil), (q_ref, k_ref, v_ref) in zip(DILATED_PATTERNS, views):
        if dil == 1:
            run_pattern(q_ref, k_ref, v_ref, dil, None, None)


def _attn_call(q_views, k_views, v_views):
    b, pairs, s, _ = q_views[0].shape
    in_specs, args = [], []
    for pi in range(len(DILATED_PATTERNS)):
        for views in (q_views, k_views, v_views):
            in_specs.append(pl.BlockSpec((1, 1) + views[pi].shape[2:], lambda i, j: (i, j, 0, 0)))
            args.append(views[pi])
    masks = _attn_masks()
    in_specs += [pl.BlockSpec(m.shape, lambda i, j: (0, 0)) for m in masks]
    return pl.pallas_call(
        functools.partial(_attn_body, s),
        grid=(b, pairs),
        in_specs=in_specs,
        out_specs=pl.BlockSpec((1, 1, s, LANES), lambda i, j: (i, j, 0, 0)),
        out_shape=jax.ShapeDtypeStruct((b, pairs, s, LANES), BF16),
        scratch_shapes=[pltpu.VMEM((s, LANES), F32)] * (2 * (len(DILATED_PATTERNS) - 1)),
        compiler_params=pltpu.CompilerParams(
            dimension_semantics=("parallel", "parallel"), vmem_limit_bytes=V7X_VMEM_LIMIT_BYTES),
        name="dilated_attn",
    )(*args, *masks)


(P_MU_R, P_MU_K, P_MU_V, P_MU_W, P_MU_A, P_MU_G, P_W0, P_A0, P_KK, P_KA, P_RK, P_LNW, P_LNB) = range(13)
P_ROWS = 16


def _rwkv_consts():
    c = RWKV_CHUNK
    assert c == HEAD_DIM
    t = np.arange(SLAB)
    tri = ((t[:, None] // c == t[None, :] // c) & (t[None, :] <= t[:, None]))
    blockdiag = (np.arange(SLAB)[:, None] // HEAD_DIM == np.arange(SLAB)[None, :] // HEAD_DIM)
    src = np.arange(SLAB_HEADS * c) % c
    tok = np.arange(c)[:, None]
    return (jnp.asarray(tri, BF16),
            jnp.asarray(blockdiag, F32),
            jnp.asarray(blockdiag, BF16),
            jnp.asarray(src[None, :] < tok, F32),
            jnp.asarray(src[None, :] <= tok, F32),
            jnp.asarray(src[None, :] == tok, F32))


def _rwkv_body(rin_ref, p_ref, w1_ref, w2_ref, a1_ref, a2_ref, g1_ref, g2_ref,
               tri_ref, bd_ref, bd16_ref, lt_ref, le_ref, eye_ref, o_ref, state_ref, prev_ref):
    c = RWKV_CHUNK
    nb, tl, w4 = rin_ref.shape
    w = o_ref.shape[-1]
    n_slabs = w // SLAB
    n_sub = tl // c
    step = pl.program_id(1)

    @pl.when(step == 0)
    def _():
        state_ref[...] = jnp.zeros_like(state_ref)
        prev_ref[...] = jnp.zeros_like(prev_ref)

    x = rin_ref[...].reshape(nb * tl, w4)
    row = lax.broadcasted_iota(jnp.int32, (nb * tl, 1), 0)
    x_prev = pltpu.roll(x, 1, 0)
    for bi in range(nb):
        x_prev = jnp.where(row == bi * tl, prev_ref[bi, 0:1, :], x_prev)
    for bi in range(nb):
        prev_ref[bi, 0:1, :] = rin_ref[bi, tl - 1:tl, :]

    def par(prow):
        return p_ref[prow:prow + 1, :]

    def sect(a, i):
        return a[:, i * w:(i + 1) * w]

    def lerp(i, mu_row):
        cur, prev = sect(x, i), sect(x_prev, i)
        return cur + (prev - cur) * par(mu_row)

    r = lerp(0, P_MU_R)
    k = lerp(1, P_MU_K)
    v = lerp(2, P_MU_V)
    cw = lerp(3, P_MU_W)
    ca = lerp(3, P_MU_A)
    cg = lerp(3, P_MU_G)

    bd = bd_ref[...]
    bd16 = bd16_ref[...]

    def head_sum(z):
        return _head_sums(z, bd16)

    zw = par(P_W0) + _dot(_bf(jnp.tanh(_dot(_bf(cw), w1_ref[...]))), w2_ref[...])
    log_w = -math.exp(-0.5) * jax.nn.sigmoid(zw)
    a_gate = jax.nn.sigmoid(par(P_A0) + _dot(_bf(_dot(_bf(ca), a1_ref[...])), a2_ref[...]))
    gate = _dot(_bf(jax.nn.sigmoid(_dot(_bf(cg), g1_ref[...]))), g2_ref[...])

    kk = k * par(P_KK)
    kk = kk / jnp.maximum(jnp.sqrt(head_sum(kk * kk)), 1e-12)
    k = k * (1.0 + (a_gate - 1.0) * par(P_KA))
    b_vec = kk * a_gate
    bonus = head_sum(r * k * par(P_RK)) * v

    tri = tri_ref[...]
    cum = jnp.concatenate([_split_dot_left(tri, log_w[i:i + SLAB], 2) for i in range(0, nb * tl, SLAB)], axis=0)
    strictly_lower = lt_ref[...] > 0.0
    lower = le_ref[...] > 0.0
    eye = eye_ref[...]

    def block_diag(z):
        return jnp.concatenate([_bf(z)] * SLAB_HEADS, axis=0) * bd16

    probs = [(bi, ci, si) for bi in range(nb) for ci in range(n_sub) for si in range(n_slabs)]
    pre = {}
    for bi in range(nb):
        for ci in range(n_sub):
            rows = slice((bi * n_sub + ci) * c, (bi * n_sub + ci + 1) * c)
            lw_c, cum_c = log_w[rows], cum[rows]
            total = jnp.sum(lw_c, axis=0, keepdims=True)
            grow = jnp.exp(-cum_c)
            dec_out = jnp.exp(total - cum_c)
            a_t = -kk[rows] * jnp.exp(cum_c - lw_c)
            r_t = r[rows] * jnp.exp(cum_c)
            b_t, k_t = b_vec[rows] * grow, k[rows] * grow
            b_o, k_o = b_vec[rows] * dec_out, k[rows] * dec_out
            w_rows = jnp.broadcast_to(jnp.exp(total), (2 * c, w))
            for si in range(n_slabs):
                ln = slice(si * SLAB, (si + 1) * SLAB)
                ar = _bf(jnp.concatenate([a_t[:, ln], r_t[:, ln]], axis=0))
                g_b = _dot_nt(ar, block_diag(b_t[:, ln]))
                g_k = _dot_nt(ar, block_diag(k_t[:, ln]))
                a_k = jnp.concatenate([jnp.where(strictly_lower, g_k[:c], 0.0),
                                       jnp.where(lower, g_k[c:], 0.0)], axis=0)
                turned = jnp.transpose(jnp.concatenate([b_o[:, ln], k_o[:, ln], w_rows[:, ln]], axis=0))
                pre[bi, ci, si] = dict(
                    ar=ar, v=v[rows, ln],
                    a_ab=jnp.where(strictly_lower, g_b[:c], 0.0),
                    a_rb=_bf(jnp.where(lower, g_b[c:], 0.0)),
                    from_v=_dot(_bf(a_k), block_diag(v[rows, ln])),
                    out_t=_bf(turned[:, :2 * c]),
                    w_col=turned[:, 2 * c:2 * c + 1])

    inv = {p: eye + pre[p]["a_ab"] for p in probs}
    power = {p: _dot(_bf(pre[p]["a_ab"]), block_diag(pre[p]["a_ab"])) for p in probs}
    covered = 2
    while covered * 2 < c:
        for p in probs:
            both = _dot(_bf(jnp.concatenate([inv[p], power[p]], axis=0)), block_diag(power[p]))
            inv[p] = inv[p] + both[:c]
            power[p] = both[c:]
        covered *= 2
    for p in probs:
        inv[p] = _bf(inv[p] + _dot(_bf(inv[p]), block_diag(power[p])))

    states = {(bi, si): state_ref[bi * n_slabs + si] for bi in range(nb) for si in range(n_slabs)}
    y_parts = {}
    chains = list(states)
    for ci in range(n_sub):
        pp = {ch: pre[ch[0], ci, ch[1]] for ch in chains}
        from_state = {ch: _dot(pp[ch]["ar"], _bf(states[ch])) for ch in chains}
        rhs = {ch: block_diag(from_state[ch][:c] + pp[ch]["from_v"][:c]) for ch in chains}
        u = {ch: _dot(inv[ch[0], ci, ch[1]], rhs[ch]) for ch in chains}
        for ch in chains:
            states[ch] = states[ch] * pp[ch]["w_col"] + bd * _dot(
                pp[ch]["out_t"], _bf(jnp.concatenate([u[ch], pp[ch]["v"]], axis=0)))
        for ch in chains:
            y_parts[ch[0], ci, ch[1]] = (from_state[ch][c:] + pp[ch]["from_v"][c:]
                                         + _dot(pp[ch]["a_rb"], block_diag(u[ch])))
    for bi in range(nb):
        for si in range(n_slabs):
            state_ref[bi * n_slabs + si] = states[bi, si]
    y = jnp.concatenate(
        [jnp.concatenate([y_parts[bi, ci, si] for si in range(n_slabs)], axis=1)
         for bi in range(nb) for ci in range(n_sub)], axis=0)

    inv_n = 1.0 / HEAD_DIM
    centered = y - head_sum(y) * inv_n
    var = head_sum(centered * centered) * inv_n
    y = centered * lax.rsqrt(var + GN_EPS) * par(P_LNW) + par(P_LNB)
    o_ref[...] = _bf((y + bonus) * gate).reshape(nb, tl, w)


def _pad_lora(down, up):
    rank = down.shape[1]
    return (_bf(jnp.pad(down, ((0, 0), (0, LORA_PAD - rank)))),
            _bf(jnp.pad(up, ((0, LORA_PAD - rank), (0, 0)))))


def _rwkv_call(rin, chan_params, loras):
    b, s, w4 = rin.shape
    w = w4 // 4
    tl = RWKV_TOKENS
    nb = RWKV_SEQS if b % RWKV_SEQS == 0 else 1
    packed = jnp.zeros((P_ROWS, w), F32).at[:len(chan_params)].set(jnp.stack(chan_params))
    lora_args = []
    for down, up in loras:
        lora_args += list(_pad_lora(down, up))
    assert tl % SLAB == 0
    consts = _rwkv_consts()
    fixed = lambda i, j: (0, 0)
    return pl.pallas_call(
        _rwkv_body,
        grid=(b // nb, s // tl),
        in_specs=[pl.BlockSpec((nb, tl, w4), lambda i, j: (i, j, 0)), pl.BlockSpec((P_ROWS, w), fixed)]
        + [pl.BlockSpec(a.shape, fixed) for a in lora_args]
        + [pl.BlockSpec(a.shape, fixed) for a in consts],
        out_specs=pl.BlockSpec((nb, tl, w), lambda i, j: (i, j, 0)),
        out_shape=jax.ShapeDtypeStruct((b, s, w), BF16),
        scratch_shapes=[pltpu.VMEM((nb * (w // SLAB), SLAB, SLAB), F32), pltpu.VMEM((nb, 8, w4), F32)],
        compiler_params=pltpu.CompilerParams(
            dimension_semantics=("parallel", "arbitrary"), vmem_limit_bytes=V7X_VMEM_LIMIT_BYTES),
        name="rwkv7",
    )(rin, packed, *lora_args, *consts)


def kernel(x, ffn1_norm, ffn1_w_gate, ffn1_w_up, ffn1_w_down, mix_norm, w_in, q_norm, k_norm, mu_r, mu_k, mu_v, mu_w, mu_a, mu_g, w0, w1, w2, a0, a1, a2, g1, g2, k_k, k_a, r_k, ln_x_w, ln_x_b, w_out, ffn2_norm, ffn2_w_gate, ffn2_w_up, ffn2_w_down):
    b, s, d = x.shape
    depth = w_in.shape[0]
    rw_width = mu_r.shape[-1]
    att_width = (w_in.shape[-1] - 4 * rw_width) // 3
    h = x.reshape(b * s, d)
    for l in range(depth):
        h = _ffn_call(h, ffn1_norm[l], ffn1_w_gate, ffn1_w_up, ffn1_w_down, l)
        outs = _in_proj_call(h.reshape(b, s, d), mix_norm[l], w_in, l, q_norm[l], k_norm[l], att_width)
        att = _attn_call(outs[0:3], outs[3:6], outs[6:9])
        rin = outs[9]
        chan = [mu_r[l], mu_k[l], mu_v[l], mu_w[l], mu_a[l], mu_g[l], w0[l], a0[l], k_k[l], k_a[l],
                r_k[l].reshape(rw_width), ln_x_w[l], ln_x_b[l]]
        rw = _rwkv_call(rin, chan, [(w1[l], w2[l]), (a1[l], a2[l]), (g1[l], g2[l])])
        h = _ffn_call(h, ffn2_norm[l], ffn2_w_gate, ffn2_w_up, ffn2_w_down, l,
                      mix=(att, rw.reshape(b * s, rw_width), w_out))
    return h.reshape(b, s, d)
```

```python
import functools
import math

import jax
import jax.numpy as jnp
import numpy as np
from jax import lax
from jax.experimental import pallas as pl
from jax.experimental.pallas import tpu as pltpu

F32 = jnp.float32
BF16 = jnp.bfloat16

HEAD_DIM = 64
LANES = 128
DILATED_PATTERNS = ((128, 1), (512, 4), (2048, 16))
ATT_WINDOW = 128
ATT_GROUP = 4
ATT_LOOP_GROUP = 5
FFN_RESIDUAL = 0.5
RMS_EPS = 1e-6
GN_EPS = 64e-5
NEG_INF = -1e30
LORA_PAD = 128

V7X_VMEM_LIMIT_BYTES = 60000 * 1024

FFN_TOKENS = 1024
FFN_HIDDEN_CHUNK = 256
WEIGHT_LOAD_ROWS = 256
IN_TOKENS = 512
RWKV_TOKENS = 256
RWKV_SEQS = 2
RWKV_CHUNK = 64
SLAB_HEADS = 4
SLAB = SLAB_HEADS * HEAD_DIM


def _bf(x):
    return x.astype(BF16)


def _dot(a, b):
    return jnp.dot(a, b, preferred_element_type=F32)


def _dot_nt(a, b):
    return lax.dot_general(a, b, (((1,), (1,)), ((), ())), preferred_element_type=F32)


def _dot_tn(a, b):
    return lax.dot_general(a, b, (((0,), (0,)), ((), ())), preferred_element_type=F32)


def _split_dot_left(w, x, parts):
    out = None
    rem = x
    for p in range(parts):
        piece = _bf(rem)
        term = _dot(w, piece)
        out = term if out is None else out + term
        if p + 1 < parts:
            rem = rem - piece.astype(F32)
    return out


def _head_sums(z, membership):
    g = membership.shape[0]
    return jnp.concatenate([_dot(_bf(z[:, i:i + g]), membership) for i in range(0, z.shape[1], g)], axis=1)


def _rms_norm_rows(x, gain):
    ms = jnp.mean(x * x, axis=-1, keepdims=True)
    return x * lax.rsqrt(ms + RMS_EPS) * gain


def _load_as_bf16(src_hbm, dst_ref, stage_ref, sem_ref):
    n_rows, cols = src_hbm.shape
    rows = stage_ref.shape[1]
    assert n_rows % rows == 0
    n = n_rows // rows

    def copy(i):
        slot = i % 2
        return pltpu.make_async_copy(src_hbm.at[pl.ds(i * rows, rows), :],
                                     stage_ref.at[slot, :, pl.ds(0, cols)], sem_ref.at[slot])

    copy(0).start()
    for i in range(n):
        if i + 1 < n:
            copy(i + 1).start()
        copy(i).wait()
        dst_ref[pl.ds(i * rows, rows), :] = _bf(stage_ref[i % 2, :, pl.ds(0, cols)])


def _ffn_body(with_mix, ff_chunks, layer, *refs):
    if with_mix:
        (x_ref, att_ref, rw_ref, nw_ref, wo_hbm, wg_hbm, wu_hbm, wd_hbm, o_ref,
         wo_ref, wg_ref, wu_ref, wd_ref, stage_ref, sem_ref) = refs
    else:
        x_ref, nw_ref, wg_hbm, wu_hbm, wd_hbm, o_ref, wg_ref, wu_ref, wd_ref, stage_ref, sem_ref = refs

    @pl.when(pl.program_id(0) == 0)
    def _():
        if with_mix:
            _load_as_bf16(wo_hbm.at[layer], wo_ref, stage_ref, sem_ref)
        _load_as_bf16(wg_hbm.at[layer], wg_ref, stage_ref, sem_ref)
        _load_as_bf16(wu_hbm.at[layer], wu_ref, stage_ref, sem_ref)
        _load_as_bf16(wd_hbm.at[layer], wd_ref, stage_ref, sem_ref)

    x = x_ref[...]
    if with_mix:
        mixed = jnp.concatenate([att_ref[0, p] for p in range(att_ref.shape[1])] + [rw_ref[...]], axis=1)
        x = x + _dot(mixed, wo_ref[...])
    h = _bf(_rms_norm_rows(x, nw_ref[...]))
    acc = None
    for lo, hi in ff_chunks:
        g = _dot(h, wg_ref[:, lo:hi])
        u = _dot(h, wu_ref[:, lo:hi])
        part = _dot(_bf(g * jax.nn.sigmoid(g) * u), wd_ref[lo:hi, :])
        acc = part if acc is None else acc + part
    o_ref[...] = x + FFN_RESIDUAL * acc


def _ff_chunks(d_ff):
    bounds = list(range(0, d_ff, FFN_HIDDEN_CHUNK)) + [d_ff]
    return tuple(zip(bounds[:-1], bounds[1:]))


def _ffn_call(x, norm_w, wg, wu, wd, layer, mix=None):
    t, d = x.shape
    d_ff = wg.shape[2]
    tm = FFN_TOKENS
    row = lambda i: (i, 0)
    hbm = pl.BlockSpec(memory_space=pl.ANY)
    in_specs = [pl.BlockSpec((tm, d), row)]
    args = [x]
    scratch = []
    if mix is not None:
        att, rw, w_out = mix
        _, pairs, s, _ = att.shape
        tiles = s // tm
        in_specs += [pl.BlockSpec((1, pairs, tm, LANES), lambda i: (i // tiles, 0, i % tiles, 0)),
                     pl.BlockSpec((tm, rw.shape[1]), row)]
        args += [att, rw]
    in_specs.append(pl.BlockSpec((1, d), lambda i: (0, 0)))
    args.append(norm_w.reshape(1, d))
    if mix is not None:
        in_specs.append(hbm)
        args.append(w_out)
        scratch.append(pltpu.VMEM(w_out.shape[1:], BF16))
    in_specs += [hbm, hbm, hbm]
    args += [wg, wu, wd]
    scratch += [pltpu.VMEM((d, d_ff), BF16), pltpu.VMEM((d, d_ff), BF16), pltpu.VMEM((d_ff, d), BF16),
                pltpu.VMEM((2, WEIGHT_LOAD_ROWS, max(d, d_ff)), F32), pltpu.SemaphoreType.DMA((2,))]
    return pl.pallas_call(
        functools.partial(_ffn_body, mix is not None, _ff_chunks(d_ff), layer),
        grid=(t // tm,),
        in_specs=in_specs,
        out_specs=pl.BlockSpec((tm, d), row),
        out_shape=jax.ShapeDtypeStruct((t, d), F32),
        scratch_shapes=scratch,
        compiler_params=pltpu.CompilerParams(
            dimension_semantics=("arbitrary",), vmem_limit_bytes=V7X_VMEM_LIMIT_BYTES),
        name="ffn_mix" if mix is not None else "ffn",
    )(*args)


def _in_proj_body(att_width, layer, x_ref, nw_ref, w_hbm, qg_ref, kg_ref, hs_ref, *out_refs):
    view_refs, rin_ref, stage_refs = out_refs[:9], out_refs[9], out_refs[10:-3]
    w_ref, w_stage_ref, sem_ref = out_refs[-3:]

    @pl.when((pl.program_id(0) == 0) & (pl.program_id(1) == 0))
    def _():
        _load_as_bf16(w_hbm.at[layer], w_ref, w_stage_ref, sem_ref)

    h = _bf(_rms_norm_rows(x_ref[0], nw_ref[...]))
    aw = att_width
    proj = _dot(h, w_ref[:, :3 * aw])
    hs = hs_ref[...]
    tm = proj.shape[0]

    def head_rms(z, gain):
        ms = _head_sums(z * z, hs) * (1.0 / HEAD_DIM)
        return z * lax.rsqrt(ms + RMS_EPS) * gain

    tensors = (head_rms(proj[:, 0:aw], qg_ref[...]) * (HEAD_DIM ** -0.5),
               head_rms(proj[:, aw:2 * aw], kg_ref[...]),
               proj[:, 2 * aw:3 * aw])
    rw_cols = rin_ref.shape[-1]
    rw_step = rw_cols // (len(tensors) + 1)

    def rw_part(i):
        cols = slice(i * rw_step, (i + 1) * rw_step)
        rin_ref[0, :, cols] = _dot(h, w_ref[:, 3 * aw + cols.start:3 * aw + cols.stop])

    pairs = aw // LANES
    dils = [dil for _, dil in DILATED_PATTERNS]
    assert dils[0] == 1 and len(stage_refs) == len(dils) - 1
    rw_part(len(tensors))
    for ti, val in enumerate(tensors):
        rw_part(ti)
        for p in range(pairs):
            nat = val[:, p * LANES:(p + 1) * LANES]
            stage_refs[0][p] = nat
            view_refs[3 * ti][0, p] = _bf(nat)
        for li in range(1, len(dils)):
            prev_dil, dil = dils[li - 1], dils[li]
            ratio = dil // prev_dil
            for p in range(pairs):
                for cp in range(prev_dil):
                    for c2 in range(ratio):
                        c = cp + prev_dil * c2
                        blk = stage_refs[li - 1][p * prev_dil + cp, pl.ds(c2, tm // dil, stride=ratio), :]
                        if li + 1 < len(dils):
                            stage_refs[li][p * dil + c] = blk
                        view_refs[3 * ti + li][0, p, :, c * LANES:(c + 1) * LANES] = _bf(blk)


def _head_sum_matrix(width):
    head = np.arange(width) // HEAD_DIM
    return jnp.asarray(head[:, None] == head[None, :], dtype=BF16)


def _in_proj_call(x, norm_w, w_in, layer, q_gain, k_gain, att_width):
    b, s, d = x.shape
    cols = w_in.shape[2]
    rw_cols = cols - 3 * att_width
    tm = IN_TOKENS
    heads = att_width // HEAD_DIM
    pairs = att_width // LANES
    fixed = lambda i, j: (0, 0)
    view_shapes, view_specs = [], []
    for _, dil in DILATED_PATTERNS:
        view_shapes.append(jax.ShapeDtypeStruct((b, pairs, s // dil, dil * LANES), BF16))
        view_specs.append(pl.BlockSpec((1, pairs, tm // dil, dil * LANES), lambda i, j: (i, 0, j, 0)))
    return pl.pallas_call(
        functools.partial(_in_proj_body, att_width, layer),
        grid=(b, s // tm),
        in_specs=[pl.BlockSpec((1, tm, d), lambda i, j: (i, j, 0)), pl.BlockSpec((1, d), fixed),
                  pl.BlockSpec(memory_space=pl.ANY),
                  pl.BlockSpec((1, att_width), fixed), pl.BlockSpec((1, att_width), fixed),
                  pl.BlockSpec((SLAB, SLAB), fixed)],
        out_specs=view_specs * 3 + [pl.BlockSpec((1, tm, rw_cols), lambda i, j: (i, j, 0))],
        out_shape=view_shapes * 3 + [jax.ShapeDtypeStruct((b, s, rw_cols), F32)],
        scratch_shapes=[pltpu.VMEM((pairs * dil, tm // dil, LANES), F32) for _, dil in DILATED_PATTERNS[:-1]]
        + [pltpu.VMEM((d, cols), BF16), pltpu.VMEM((2, WEIGHT_LOAD_ROWS, cols), F32),
           pltpu.SemaphoreType.DMA((2,))],
        compiler_params=pltpu.CompilerParams(
            dimension_semantics=("arbitrary", "arbitrary"), vmem_limit_bytes=V7X_VMEM_LIMIT_BYTES),
        name="in_proj",
    )(x, norm_w.reshape(1, d), w_in, jnp.tile(q_gain, heads).reshape(1, att_width),
      jnp.tile(k_gain, heads).reshape(1, att_width), _head_sum_matrix(SLAB))


def _attn_masks():
    w = ATT_WINDOW
    r = np.arange(w)[:, None]
    causal = np.arange(w)[None, :] <= r
    u = np.arange(2 * w)[None, :]
    band = (u >= r) & (u <= r + w)
    two_heads = lambda m: jnp.asarray(np.concatenate([m, m], axis=0), F32)
    return two_heads(causal), two_heads(band)


def _attn_body(seq, qn_ref, kn_ref, vn_ref, q4_ref, k4_ref, v4_ref, q16_ref, k16_ref, v16_ref,
               causal_ref, band_ref, o_ref, *part_refs):
    w = ATT_WINDOW
    lower_head = lax.broadcasted_iota(jnp.int32, (1, LANES), 1) < HEAD_DIM
    causal = causal_ref[...] > 0.0
    band = band_ref[...] > 0.0

    def pick(two):
        return jnp.where(lower_head, two[:w], two[w:])

    def blocks(tasks):
        qq = []
        for q, _, _, _ in tasks:
            zero = jnp.zeros_like(q)
            qq.append(jnp.concatenate([jnp.where(lower_head, q, zero), jnp.where(lower_head, zero, q)], axis=0))
        s = [jnp.where(t[3], _dot_nt(a, t[1]), NEG_INF) for a, t in zip(qq, tasks)]
        m = [jnp.max(x, axis=1, keepdims=True) for x in s]
        p = [_bf(jnp.exp(x - mx)) for x, mx in zip(s, m)]
        ol = [_dot(x, jnp.concatenate([t[2], jnp.ones_like(t[2])], axis=1)) for x, t in zip(p, tasks)]
        results = []
        for oli, mi in zip(ol, m):
            l_full = pick(oli[:, LANES:])
            m_full = pick(jnp.broadcast_to(mi, (2 * w, LANES)))
            results.append((pick(oli[:, :LANES]) / l_full, m_full + jnp.log(l_full)))
        return results

    def run_pattern(q_ref, k_ref, v_ref, dil, out_ref, lse_ref):
        n_blocks = seq // dil // w

        def offset(nb, size):
            return nb * size if isinstance(nb, int) else pl.multiple_of(nb * size, size)

        def task(c, nb):
            lanes = slice(c * LANES, (c + 1) * LANES)
            if isinstance(nb, int) and nb == 0:
                rows = slice(0, w)
                return q_ref[0, 0, rows, lanes], k_ref[0, 0, rows, lanes], v_ref[0, 0, rows, lanes], causal
            q_rows = pl.ds(offset(nb, w), w)
            k_rows = pl.ds(offset(nb - 1, w), 2 * w)
            return q_ref[0, 0, q_rows, lanes], k_ref[0, 0, k_rows, lanes], v_ref[0, 0, k_rows, lanes], band

        def store(c, nb, result):
            if dil > 1:
                rows = pl.ds(nb * (w * dil) + c, w, stride=dil)
                out_ref[rows, :], lse_ref[rows, :] = result
            else:
                merge(pl.ds(offset(nb, w), w), *result)

        def run(group):
            for (c, nb), result in zip(group, blocks([task(c, nb) for c, nb in group])):
                store(c, nb, result)

        looped = n_blocks > ATT_GROUP
        static = [(c, nb) for c in range(dil) for nb in range(1 if looped else n_blocks)]
        group = ATT_GROUP * (2 if n_blocks == 1 else 1)
        for g in range(0, len(static), group):
            run(static[g:g + group])
        if looped:
            assert (n_blocks - 1) % ATT_LOOP_GROUP == 0
            for c in range(dil):
                def trip(i, carry):
                    run([(c, 1 + i * ATT_LOOP_GROUP + t) for t in range(ATT_LOOP_GROUP)])
                    return carry

                lax.fori_loop(0, (n_blocks - 1) // ATT_LOOP_GROUP, trip, 0)

    def merge(rows, out_nat, lse_nat):
        outs = [out_nat] + [part_refs[2 * pi][rows, :] for pi in range(n_strided)]
        lses = [lse_nat] + [part_refs[2 * pi + 1][rows, :] for pi in range(n_strided)]
        top = functools.reduce(jnp.maximum, lses)
        wts = [jnp.exp(l - top) for l in lses]
        o_ref[0, 0, rows, :] = _bf(sum(wt * o for wt, o in zip(wts, outs)) / sum(wts))

    views = ((qn_ref, kn_ref, vn_ref), (q4_ref, k4_ref, v4_ref), (q16_ref, k16_ref, v16_ref))
    strided = [(dil, v) for (window, dil), v in zip(DILATED_PATTERNS, views) if dil > 1]
    n_strided = len(strided)
    assert all(window // dil == w for window, dil in DILATED_PATTERNS) and n_strided == len(views) - 1
    for pi, (dil, (q_ref, k_ref, v_ref)) in enumerate(strided):
        run_pattern(q_ref, k_ref, v_ref, dil, part_refs[2 * pi], part_refs[2 * pi + 1])
    for (window, dil), (q_ref, k_ref, v_ref) in zip(DILATED_PATTERNS, views):
        if dil == 1:
            run_pattern(q_ref, k_ref, v_ref, dil, None, None)


def _attn_call(q_views, k_views, v_views):
    b, pairs, s, _ = q_views[0].shape
    in_specs, args = [], []
    for pi in range(len(DILATED_PATTERNS)):
        for views in (q_views, k_views, v_views):
            in_specs.append(pl.BlockSpec((1, 1) + views[pi].shape[2:], lambda i, j: (i, j, 0, 0)))
            args.append(views[pi])
    masks = _attn_masks()
    in_specs += [pl.BlockSpec(m.shape, lambda i, j: (0, 0)) for m in masks]
    return pl.pallas_call(
        functools.partial(_attn_body, s),
        grid=(b, pairs),
        in_specs=in_specs,
        out_specs=pl.BlockSpec((1, 1, s, LANES), lambda i, j: (i, j, 0, 0)),
        out_shape=jax.ShapeDtypeStruct((b, pairs, s, LANES), BF16),
        scratch_shapes=[pltpu.VMEM((s, LANES), F32)] * (2 * (len(DILATED_PATTERNS) - 1)),
        compiler_params=pltpu.CompilerParams(
            dimension_semantics=("parallel", "parallel"), vmem_limit_bytes=V7X_VMEM_LIMIT_BYTES),
        name="dilated_attn",
    )(*args, *masks)


(P_MU_R, P_MU_K, P_MU_V, P_MU_W, P_MU_A, P_MU_G, P_W0, P_A0, P_KK, P_KA, P_RK, P_LNW, P_LNB) = range(13)
P_ROWS = 16


def _rwkv_consts():
    c = RWKV_CHUNK
    assert c == HEAD_DIM
    t = np.arange(SLAB)
    tri = ((t[:, None] // c == t[None, :] // c) & (t[None, :] <= t[:, None]))
    blockdiag = (np.arange(SLAB)[:, None] // HEAD_DIM == np.arange(SLAB)[None, :] // HEAD_DIM)
    src = np.arange(SLAB_HEADS * c) % c
    tok = np.arange(c)[:, None]
    return (jnp.asarray(tri, BF16),
            jnp.asarray(blockdiag, F32),
            jnp.asarray(blockdiag, BF16),
            jnp.asarray(src[None, :] < tok, F32),
            jnp.asarray(src[None, :] <= tok, F32),
            jnp.asarray(src[None, :] == tok, F32))


def _rwkv_body(rin_ref, p_ref, w1_ref, w2_ref, a1_ref, a2_ref, g1_ref, g2_ref,
               tri_ref, bd_ref, bd16_ref, lt_ref, le_ref, eye_ref, o_ref, state_ref, prev_ref):
    c = RWKV_CHUNK
    nb, tl, w4 = rin_ref.shape
    w = o_ref.shape[-1]
    n_slabs = w // SLAB
    n_sub = tl // c
    step = pl.program_id(1)

    @pl.when(step == 0)
    def _():
        state_ref[...] = jnp.zeros_like(state_ref)
        prev_ref[...] = jnp.zeros_like(prev_ref)

    x = rin_ref[...].reshape(nb * tl, w4)
    row = lax.broadcasted_iota(jnp.int32, (nb * tl, 1), 0)
    x_prev = pltpu.roll(x, 1, 0)
    for bi in range(nb):
        x_prev = jnp.where(row == bi * tl, prev_ref[bi, 0:1, :], x_prev)
    for bi in range(nb):
        prev_ref[bi, 0:1, :] = rin_ref[bi, tl - 1:tl, :]

    def par(prow):
        return p_ref[prow:prow + 1, :]

    def sect(a, i):
        return a[:, i * w:(i + 1) * w]

    def lerp(i, mu_row):
        cur, prev = sect(x, i), sect(x_prev, i)
        return cur + (prev - cur) * par(mu_row)

    r = lerp(0, P_MU_R)
    k = lerp(1, P_MU_K)
    v = lerp(2, P_MU_V)
    cw = lerp(3, P_MU_W)
    ca = lerp(3, P_MU_A)
    cg = lerp(3, P_MU_G)

    bd = bd_ref[...]
    bd16 = bd16_ref[...]

    def head_sum(z):
        return _head_sums(z, bd16)

    zw = par(P_W0) + _dot(_bf(jnp.tanh(_dot(_bf(cw), w1_ref[...]))), w2_ref[...])
    log_w = -math.exp(-0.5) * jax.nn.sigmoid(zw)
    a_gate = jax.nn.sigmoid(par(P_A0) + _dot(_bf(_dot(_bf(ca), a1_ref[...])), a2_ref[...]))
    gate = _dot(_bf(jax.nn.sigmoid(_dot(_bf(cg), g1_ref[...]))), g2_ref[...])

    kk = k * par(P_KK)
    kk = kk / jnp.maximum(jnp.sqrt(head_sum(kk * kk)), 1e-12)
    k = k * (1.0 + (a_gate - 1.0) * par(P_KA))
    b_vec = kk * a_gate
    bonus = head_sum(r * k * par(P_RK)) * v

    tri = tri_ref[...]
    cum = jnp.concatenate([_split_dot_left(tri, log_w[i:i + SLAB], 2) for i in range(0, nb * tl, SLAB)], axis=0)
    strictly_lower = lt_ref[...] > 0.0
    lower = le_ref[...] > 0.0
    eye = eye_ref[...]

    def block_diag(z):
        return jnp.concatenate([_bf(z)] * SLAB_HEADS, axis=0) * bd16

    probs = [(bi, ci, si) for bi in range(nb) for ci in range(n_sub) for si in range(n_slabs)]
    pre = {}
    for bi in range(nb):
        for ci in range(n_sub):
            rows = slice((bi * n_sub + ci) * c, (bi * n_sub + ci + 1) * c)
            lw_c, cum_c = log_w[rows], cum[rows]
            total = jnp.sum(lw_c, axis=0, keepdims=True)
            grow = jnp.exp(-cum_c)
            dec_out = jnp.exp(total - cum_c)
            a_t = -kk[rows] * jnp.exp(cum_c - lw_c)
            r_t = r[rows] * jnp.exp(cum_c)
            b_t, k_t = b_vec[rows] * grow, k[rows] * grow
            b_o, k_o = b_vec[rows] * dec_out, k[rows] * dec_out
            w_rows = jnp.broadcast_to(jnp.exp(total), (2 * c, w))
            for si in range(n_slabs):
                ln = slice(si * SLAB, (si + 1) * SLAB)
                turned = jnp.transpose(jnp.concatenate([b_o[:, ln], k_o[:, ln], w_rows[:, ln]], axis=0))
                pre[bi, ci, si] = dict(
                    ar=_bf(jnp.concatenate([a_t[:, ln], r_t[:, ln]], axis=0)),
                    bd_b=block_diag(b_t[:, ln]), bd_k=block_diag(k_t[:, ln]), bd_v=block_diag(v[rows, ln]),
                    v=v[rows, ln],
                    out_t=_bf(turned[:, :2 * c]),
                    w_col=turned[:, 2 * c:2 * c + 1])
    g_b = {p: _dot_nt(pre[p]["ar"], pre[p]["bd_b"]) for p in probs}
    g_k = {p: _dot_nt(pre[p]["ar"], pre[p]["bd_k"]) for p in probs}
    for p in probs:
        a_k = jnp.concatenate([jnp.where(strictly_lower, g_k[p][:c], 0.0), jnp.where(lower, g_k[p][c:], 0.0)],
                              axis=0)
        pre[p]["from_v"] = _dot(_bf(a_k), pre[p]["bd_v"])
        pre[p]["a_ab"] = jnp.where(strictly_lower, g_b[p][:c], 0.0)
        pre[p]["a_rb"] = _bf(jnp.where(lower, g_b[p][c:], 0.0))

    inv = {p: eye + pre[p]["a_ab"] for p in probs}
    power = {p: _dot(_bf(pre[p]["a_ab"]), block_diag(pre[p]["a_ab"])) for p in probs}
    covered = 2
    while covered * 2 < c:
        for p in probs:
            both = _dot(_bf(jnp.concatenate([inv[p], power[p]], axis=0)), block_diag(power[p]))
            inv[p] = inv[p] + both[:c]
            power[p] = both[c:]
        covered *= 2
    for p in probs:
        inv[p] = _bf(inv[p] + _dot(_bf(inv[p]), block_diag(power[p])))

    states = {(bi, si): state_ref[bi * n_slabs + si] for bi in range(nb) for si in range(n_slabs)}
    y_parts = {}
    chains = list(states)
    for ci in range(n_sub):
        pp = {ch: pre[ch[0], ci, ch[1]] for ch in chains}
        from_state = {ch: _dot(pp[ch]["ar"], _bf(states[ch])) for ch in chains}
        rhs = {ch: block_diag(from_state[ch][:c] + pp[ch]["from_v"][:c]) for ch in chains}
        u = {ch: _dot(inv[ch[0], ci, ch[1]], rhs[ch]) for ch in chains}
        for ch in chains:
            states[ch] = states[ch] * pp[ch]["w_col"] + bd * _dot(
                pp[ch]["out_t"], _bf(jnp.concatenate([u[ch], pp[ch]["v"]], axis=0)))
        for ch in chains:
            y_parts[ch[0], ci, ch[1]] = (from_state[ch][c:] + pp[ch]["from_v"][c:]
                                         + _dot(pp[ch]["a_rb"], block_diag(u[ch])))
    for bi in range(nb):
        for si in range(n_slabs):
            state_ref[bi * n_slabs + si] = states[bi, si]
    y = jnp.concatenate(
        [jnp.concatenate([y_parts[bi, ci, si] for si in range(n_slabs)], axis=1)
         for bi in range(nb) for ci in range(n_sub)], axis=0)

    inv_n = 1.0 / HEAD_DIM
    centered = y - head_sum(y) * inv_n
    var = head_sum(centered * centered) * inv_n
    y = centered * lax.rsqrt(var + GN_EPS) * par(P_LNW) + par(P_LNB)
    o_ref[...] = _bf((y + bonus) * gate).reshape(nb, tl, w)


def _pad_lora(down, up):
    rank = down.shape[1]
    return (_bf(jnp.pad(down, ((0, 0), (0, LORA_PAD - rank)))),
            _bf(jnp.pad(up, ((0, LORA_PAD - rank), (0, 0)))))


def _rwkv_call(rin, chan_params, loras):
    b, s, w4 = rin.shape
    w = w4 // 4
    tl = RWKV_TOKENS
    nb = RWKV_SEQS if b % RWKV_SEQS == 0 else 1
    packed = jnp.zeros((P_ROWS, w), F32).at[:len(chan_params)].set(jnp.stack(chan_params))
    lora_args = []
    for down, up in loras:
        lora_args += list(_pad_lora(down, up))
    assert tl % SLAB == 0
    consts = _rwkv_consts()
    fixed = lambda i, j: (0, 0)
    return pl.pallas_call(
        _rwkv_body,
        grid=(b // nb, s // tl),
        in_specs=[pl.BlockSpec((nb, tl, w4), lambda i, j: (i, j, 0)), pl.BlockSpec((P_ROWS, w), fixed)]
        + [pl.BlockSpec(a.shape, fixed) for a in lora_args]
        + [pl.BlockSpec(a.shape, fixed) for a in consts],
        out_specs=pl.BlockSpec((nb, tl, w), lambda i, j: (i, j, 0)),
        out_shape=jax.ShapeDtypeStruct((b, s, w), BF16),
        scratch_shapes=[pltpu.VMEM((nb * (w // SLAB), SLAB, SLAB), F32), pltpu.VMEM((nb, 8, w4), F32)],
        compiler_params=pltpu.CompilerParams(
            dimension_semantics=("parallel", "arbitrary"), vmem_limit_bytes=V7X_VMEM_LIMIT_BYTES),
        name="rwkv7",
    )(rin, packed, *lora_args, *consts)


def kernel(x, ffn1_norm, ffn1_w_gate, ffn1_w_up, ffn1_w_down, mix_norm, w_in, q_norm, k_norm, mu_r, mu_k, mu_v, mu_w, mu_a, mu_g, w0, w1, w2, a0, a1, a2, g1, g2, k_k, k_a, r_k, ln_x_w, ln_x_b, w_out, ffn2_norm, ffn2_w_gate, ffn2_w_up, ffn2_w_down):
    b, s, d = x.shape
    depth = w_in.shape[0]
    rw_width = mu_r.shape[-1]
    att_width = (w_in.shape[-1] - 4 * rw_width) // 3
    h = x.reshape(b * s, d)
    for l in range(depth):
        h = _ffn_call(h, ffn1_norm[l], ffn1_w_gate, ffn1_w_up, ffn1_w_down, l)
        outs = _in_proj_call(h.reshape(b, s, d), mix_norm[l], w_in, l, q_norm[l], k_norm[l], att_width)
        att = _attn_call(outs[0:3], outs[3:6], outs[6:9])
        rin = outs[9]
        chan = [mu_r[l], mu_k[l], mu_v[l], mu_w[l], mu_a[l], mu_g[l], w0[l], a0[l], k_k[l], k_a[l],
                r_k[l].reshape(rw_width), ln_x_w[l], ln_x_b[l]]
        rw = _rwkv_call(rin, chan, [(w1[l], w2[l]), (a1[l], a2[l]), (g1[l], g2[l])])
        h = _ffn_call(h, ffn2_norm[l], ffn2_w_gate, ffn2_w_up, ffn2_w_down, l,
                      mix=(att, rw.reshape(b * s, rw_width), w_out))
    return h.reshape(b, s, d)
```
